```python
import jax, jax.numpy as jnp
from jax import lax
import numpy as np

D_MODEL = 1024
BATCH = 16
SEQ = 4096
DEPTH = 1
DEC_BATCH = 128
DEC_SEQ = 8
PAST_LEN = 8192
PAGE_SIZE = 128

GLA_HEADS = 4
GLA_DV = D_MODEL // 2 // GLA_HEADS
GLA_DK = GLA_DV // 2
GLA_RANK = 16
GLA_NORMALIZER = 16.0
GLA_CHUNK = 16
NSA_HEADS = 8
NSA_DH = (D_MODEL - GLA_HEADS * GLA_DV) // NSA_HEADS
NSA_KVH = 2
NSA_G = NSA_HEADS // NSA_KVH
CMP_STRIDE = 16
CMP_LEN = 2 * CMP_STRIDE
CMP_HIDDEN = 4 * NSA_DH
SLC_BLOCK = 64
SLC_TOPN = 16
SEL_FORCE = 1.0e4
WINDOW = 512
Q_BLOCK = 128
SLC_QBLOCK = 32
D_FF = 4 * D_MODEL
EPS = 1e-6
GLA_QK = GLA_HEADS * GLA_DK
GLA_V = GLA_HEADS * GLA_DV
NSA_Q = NSA_HEADS * NSA_DH
NSA_KV = 6 * NSA_KVH * NSA_DH
NSA_GATE = 3 * NSA_HEADS
IN_SIZES = (GLA_QK, GLA_QK, GLA_V, GLA_RANK, GLA_V, NSA_Q, NSA_KV, NSA_GATE)
D_IN = 2 * GLA_QK + 2 * GLA_V + GLA_RANK + NSA_Q + NSA_KV + NSA_GATE

kernel_name = 'hymba_gla_nsa_decoder_step'


def rmsnorm(x, g):
    xf = x.astype(jnp.float32)
    y = xf * lax.rsqrt(jnp.mean(xf * xf, axis=-1, keepdims=True) + EPS)
    return (y * g.astype(jnp.float32)).astype(x.dtype)


def masked_softmax(s, mask):
    s = jnp.where(mask, s, -jnp.inf)
    m = jnp.max(s, axis=-1, keepdims=True)
    m = jnp.where(jnp.isfinite(m), m, 0.0)
    p = jnp.exp(s - m)
    return p / jnp.maximum(jnp.sum(p, axis=-1, keepdims=True), 1e-30)


def alibi_slopes():
    h = np.arange(1, NSA_HEADS + 1, dtype=np.float32)
    return jnp.asarray(np.exp2(-8.0 * h / NSA_HEADS).astype(np.float32)).reshape(NSA_KVH, NSA_G)


def gla_chunked(q, k, v, logf, s0):
    f32 = jnp.float32
    B, T, H, DK = q.shape
    DV = v.shape[-1]
    C = GLA_CHUNK
    n = -(-T // C)
    pad = ((0, 0), (0, n * C - T), (0, 0), (0, 0))

    def chunks(a):
        a = jnp.pad(a.astype(f32), pad)
        return a.reshape(B, n, C, H, a.shape[-1]).transpose(1, 0, 3, 2, 4)

    qc, kc, vc, gc = chunks(q), chunks(k), chunks(v), chunks(logf)
    b = jnp.cumsum(gc, axis=3)
    b_last = b[:, :, :, -1:, :]
    q_in = qc * jnp.exp(b)
    k_in = kc * jnp.exp(-b)
    k_out = kc * jnp.exp(b_last - b)
    decay = jnp.exp(b_last[:, :, :, 0, :])
    causal = jnp.tril(jnp.ones((C, C), dtype=bool))
    a_intra = jnp.where(causal, jnp.einsum('nbhik,nbhjk->nbhij', q_in, k_in), 0.0)
    o_intra = jnp.einsum('nbhij,nbhjv->nbhiv', a_intra, vc)

    def step(s, xs):
        q_n, k_n, v_n, d_n = xs
        o_n = jnp.einsum('bhik,bhkv->bhiv', q_n, s)
        s = s * d_n[..., None] + jnp.einsum('bhjk,bhjv->bhkv', k_n, v_n)
        return s, o_n

    s_fin, o_inter = lax.scan(step, s0.astype(f32), (q_in, k_out, vc, decay))
    o = (o_intra + o_inter).transpose(1, 0, 3, 2, 4).reshape(B, n * C, H, DV)[:, :T]
    return o, s_fin


def compress(kv, pos, w1, b1, w2, b2):
    B, L = kv.shape[:2]
    nch = L // CMP_STRIDE
    ch = kv[:, :nch * CMP_STRIDE].reshape(B, nch, CMP_STRIDE, NSA_KVH, NSA_DH)
    pos = pos.reshape(2, CMP_STRIDE, 1, NSA_DH)
    w1 = w1.reshape(2, CMP_STRIDE, NSA_DH, CMP_HIDDEN)
    z_first = jnp.einsum('bcjhd,jdf->bchf', ch + pos[0], w1[0])
    z_second = jnp.einsum('bcjhd,jdf->bchf', ch + pos[1], w1[1])
    hid = jax.nn.silu(z_first[:, :-1] + z_second[:, 1:] + b1)
    return hid @ w2 + b2


def cmp_to_slc_matrix(nc, ns):
    start = np.arange(nc) * CMP_STRIDE
    bs = np.arange(ns) * SLC_BLOCK
    ov = np.minimum(start[:, None] + CMP_LEN, bs[None, :] + SLC_BLOCK) - np.maximum(start[:, None], bs[None, :])
    return (np.clip(ov, 0, None) / CMP_LEN).astype(np.float32)


def nsa_cmp_slc(q, qpos, kc, vc, k_slc, v_slc, slopes):
    f32 = jnp.float32
    B, T = q.shape[:2]
    L = k_slc.shape[1]
    nc = kc.shape[1]
    ns = -(-L // SLC_BLOCK)
    n_sel = min(SLC_TOPN, ns)
    m_map = jnp.asarray(cmp_to_slc_matrix(nc, ns))
    cmp_end = jnp.arange(nc, dtype=jnp.int32) * CMP_STRIDE + (CMP_LEN - 1)
    pad = ((0, 0), (0, ns * SLC_BLOCK - L), (0, 0), (0, 0))
    kb = jnp.pad(k_slc, pad).reshape(B, ns, SLC_BLOCK, NSA_KVH, NSA_DH)
    vb = jnp.pad(v_slc, pad).reshape(B, ns, SLC_BLOCK, NSA_KVH, NSA_DH)
    vc32 = vc.astype(f32)
    bi = jnp.arange(B)[:, None, None, None]
    hi = jnp.arange(NSA_KVH)[None, None, :, None]
    blk = jnp.arange(ns, dtype=jnp.int32)
    offs = jnp.arange(SLC_BLOCK, dtype=jnp.int32)
    sl = slopes[None, None, :, :, None]

    def block(args):
        qb, pb = args
        qbl = qb.shape[1]
        d_c = pb[:, None] - cmp_end[None, :]
        s_c = jnp.einsum('bqhgd,bchd->bqhgc', qb, kc).astype(f32) - sl * d_c[None, :, None, None, :]
        p_c = masked_softmax(s_c, (d_c >= 0)[None, :, None, None, :])
        o_c = jnp.einsum('bqhgc,bchd->bqhgd', p_c, vc32)
        imp = jnp.einsum('bqhgc,cs->bqhs', p_c, m_map)
        cur = pb // SLC_BLOCK
        valid = blk[None, :] <= cur[:, None]
        forced = (blk[None, :] == 0) | (blk[None, :] == cur[:, None]) | (blk[None, :] == cur[:, None] - 1)
        score = jnp.where((valid & forced)[None, :, None, :], SEL_FORCE,
                          jnp.where(valid[None, :, None, :], imp, -SEL_FORCE))
        _, idx = lax.top_k(score, n_sel)
        ks = kb[bi, idx, :, hi, :].reshape(B, qbl, NSA_KVH, n_sel * SLC_BLOCK, NSA_DH)
        vs = vb[bi, idx, :, hi, :].reshape(B, qbl, NSA_KVH, n_sel * SLC_BLOCK, NSA_DH)
        kpos = (idx[..., None] * SLC_BLOCK + offs).reshape(B, qbl, NSA_KVH, n_sel * SLC_BLOCK)
        d_s = pb[None, :, None, None] - kpos
        s_s = jnp.einsum('bqhgd,bqhkd->bqhgk', qb, ks).astype(f32) - sl * d_s[:, :, :, None, :]
        p_s = masked_softmax(s_s, (d_s >= 0)[:, :, :, None, :])
        o_s = jnp.einsum('bqhgk,bqhkd->bqhgd', p_s, vs.astype(f32))
        return o_c, o_s

    qbl = SLC_QBLOCK if T % SLC_QBLOCK == 0 else T
    nb = T // qbl
    qs = q.reshape(B, nb, qbl, NSA_KVH, NSA_G, NSA_DH).swapaxes(0, 1)
    o_c, o_s = lax.map(block, (qs, qpos.reshape(nb, qbl)))
    o_c = o_c.swapaxes(0, 1).reshape(B, T, NSA_KVH, NSA_G, NSA_DH)
    o_s = o_s.swapaxes(0, 1).reshape(B, T, NSA_KVH, NSA_G, NSA_DH)
    return o_c, o_s


def nsa_window(q, qpos, k_w, v_w, slopes):
    f32 = jnp.float32
    B, T = q.shape[:2]
    p_w = k_w.shape[1] - T
    qbl = Q_BLOCK if T % Q_BLOCK == 0 else T
    nb = T // qbl
    span = WINDOW + qbl
    pad = ((0, 0), (WINDOW, 0), (0, 0), (0, 0))
    kp = jnp.pad(k_w, pad)
    vp = jnp.pad(v_w, pad)
    kpos0 = qpos[0] - p_w - WINDOW
    sl = slopes[None, None, :, :, None]

    def block(args):
        qb, pq, start = args
        kb = lax.dynamic_slice_in_dim(kp, start, span, axis=1)
        vb = lax.dynamic_slice_in_dim(vp, start, span, axis=1)
        ridx = start + jnp.arange(span, dtype=jnp.int32)
        dist = pq[:, None] - (kpos0 + ridx)[None, :]
        mask = (ridx >= WINDOW)[None, :] & (dist >= 0) & (dist <= WINDOW)
        s = jnp.einsum('bqhgd,bkhd->bqhgk', qb, kb).astype(f32) - sl * dist[None, :, None, None, :]
        p = masked_softmax(s, mask[None, :, None, None, :])
        return jnp.einsum('bqhgk,bkhd->bqhgd', p, vb.astype(f32))

    starts = p_w + jnp.arange(nb, dtype=jnp.int32) * qbl
    qs = q.reshape(B, nb, qbl, NSA_KVH, NSA_G, NSA_DH).swapaxes(0, 1)
    o = lax.map(block, (qs, qpos.reshape(nb, qbl), starts))
    return o.swapaxes(0, 1).reshape(B, T, NSA_KVH, NSA_G, NSA_DH)


def decoder_layer(x, c, past_kv, win_past, gla_s0, wbuf,
                  norm_mix_pre, norm_mix_post, norm_ffn_pre, norm_ffn_post, w_ada, b_ada, w_in,
                  gla_w_gate, gla_b_gate, gla_norm, cmp_pos, cmp_w1, cmp_b1, cmp_w2, cmp_b2,
                  w_out, w_up, w_down):
    f32 = jnp.float32
    B, T, _ = x.shape
    P = past_kv.shape[1]
    qpos = P + jnp.arange(T, dtype=jnp.int32)
    slopes = alibi_slopes()

    ada = jax.nn.silu(c) @ w_ada + b_ada
    sh_m, sc_m, gt_m, sh_f, sc_f, gt_f = [a[:, None, :] for a in jnp.split(ada, 6, axis=-1)]

    h = rmsnorm(x, norm_mix_pre) * (1.0 + sc_m) + sh_m
    z = h @ w_in
    split_points = [int(s) for s in np.cumsum(IN_SIZES)[:-1]]
    zq, zk, zv, za, zr, zqn, zkv, zg = jnp.split(z, split_points, axis=-1)

    q_g = zq.reshape(B, T, GLA_HEADS, GLA_DK) * (GLA_DK ** -0.5)
    k_g = zk.reshape(B, T, GLA_HEADS, GLA_DK)
    v_g = zv.reshape(B, T, GLA_HEADS, GLA_DV)
    logf = jax.nn.log_sigmoid((za @ gla_w_gate + gla_b_gate).astype(f32)) / GLA_NORMALIZER
    o_g, s_new = gla_chunked(q_g, k_g, v_g, logf.reshape(B, T, GLA_HEADS, GLA_DK), gla_s0)
    o_g = rmsnorm(o_g.astype(x.dtype), gla_norm) * jax.nn.silu(zr).reshape(B, T, GLA_HEADS, GLA_DV)
    o_g = o_g.reshape(B, T, GLA_V)

    q_n = zqn.reshape(B, T, NSA_KVH, NSA_G, NSA_DH) * (NSA_DH ** -0.5)
    kv_new = zkv.reshape(B, T, 6, NSA_KVH, NSA_DH)
    rows_new = kv_new[:, :, :4]
    kv_all = jnp.concatenate([past_kv.astype(x.dtype), rows_new], axis=1)
    kc = compress(kv_all[:, :, 0], cmp_pos[0], cmp_w1[0], cmp_b1[0], cmp_w2[0], cmp_b2[0])
    vc = compress(kv_all[:, :, 1], cmp_pos[1], cmp_w1[1], cmp_b1[1], cmp_w2[1], cmp_b2[1])
    o_c, o_s = nsa_cmp_slc(q_n, qpos, kc, vc, kv_all[:, :, 2], kv_all[:, :, 3], slopes)
    win_all = jnp.concatenate([win_past.astype(x.dtype), kv_new[:, :, 4:]], axis=1)
    o_w = nsa_window(q_n, qpos, win_all[:, :, 0], win_all[:, :, 1], slopes)
    gts = jax.nn.sigmoid(zg.astype(f32)).reshape(B, T, NSA_KVH, NSA_G, 3)
    o_n = gts[..., 0:1] * o_c + gts[..., 1:2] * o_s + gts[..., 2:3] * o_w
    o_n = o_n.astype(x.dtype).reshape(B, T, NSA_Q)

    mix = jnp.concatenate([o_g, o_n], axis=-1) @ w_out
    x = x + gt_m * rmsnorm(mix, norm_mix_post)

    h2 = rmsnorm(x, norm_ffn_pre) * (1.0 + sc_f) + sh_f
    f = jnp.square(jax.nn.relu(h2 @ w_up)) @ w_down
    x = x + gt_f * rmsnorm(f, norm_ffn_post)

    lw = win_all.shape[1]
    win_new = jnp.pad(win_all, ((0, 0), (max(wbuf - lw, 0), 0), (0, 0), (0, 0), (0, 0)))[:, -wbuf:]
    return x, rows_new, win_new, s_new.astype(x.dtype)


def setup_inputs(seed: int = 0) -> dict:
    key = jax.random.key(seed)
    ks = jax.random.split(key, 32)
    f32 = jnp.float32
    n_pages = PAST_LEN // PAGE_SIZE
    n_used = DEC_BATCH * n_pages
    n_pool = n_used + n_used // 4
    win_buf = min(WINDOW, PAST_LEN)

    def nrm(k, shape, scale):
        return scale * jax.random.normal(k, shape, f32)

    def gain(k, n):
        return 1.0 + nrm(k, (DEPTH, n), 0.1)

    page_table = jax.random.permutation(ks[3], n_pool)[:n_used].reshape(DEC_BATCH, n_pages).astype(jnp.int32)
    return {
        'x_prompt': nrm(ks[0], (BATCH, SEQ, D_MODEL), 1.0),
        'x_sample': nrm(ks[1], (DEC_BATCH, DEC_SEQ, D_MODEL), 1.0),
        'cache_kv': nrm(ks[2], (DEPTH, n_pool, PAGE_SIZE, 4, NSA_KVH, NSA_DH), 1.0),
        'state_win': nrm(ks[4], (DEPTH, DEC_BATCH, win_buf, 2, NSA_KVH, NSA_DH), 1.0),
        'state_gla': nrm(ks[5], (DEPTH, DEC_BATCH, GLA_HEADS, GLA_DK, GLA_DV), 0.5),
        'page_table': page_table,
        'c_prompt': nrm(ks[6], (BATCH, D_MODEL), 1.0),
        'c_sample': nrm(ks[7], (DEC_BATCH, D_MODEL), 1.0),
        'norm_mix_pre': gain(ks[8], D_MODEL),
        'norm_mix_post': gain(ks[9], D_MODEL),
        'norm_ffn_pre': gain(ks[10], D_MODEL),
        'norm_ffn_post': gain(ks[11], D_MODEL),
        'w_ada': nrm(ks[12], (DEPTH, D_MODEL, 6 * D_MODEL), D_MODEL ** -0.5),
        'b_ada': nrm(ks[13], (DEPTH, 6 * D_MODEL), 0.01),
        'w_in': nrm(ks[14], (DEPTH, D_MODEL, D_IN), D_MODEL ** -0.5),
        'gla_w_gate': nrm(ks[15], (DEPTH, GLA_RANK, GLA_QK), GLA_RANK ** -0.5),
        'gla_b_gate': nrm(ks[16], (DEPTH, GLA_QK), 0.01),
        'gla_norm': gain(ks[17], GLA_DV),
        'cmp_pos': nrm(ks[18], (DEPTH, 2, CMP_LEN, NSA_DH), 0.1),
        'cmp_w1': nrm(ks[19], (DEPTH, 2, CMP_LEN, NSA_DH, CMP_HIDDEN), (CMP_LEN * NSA_DH) ** -0.5),
        'cmp_b1': nrm(ks[20], (DEPTH, 2, CMP_HIDDEN), 0.01),
        'cmp_w2': nrm(ks[21], (DEPTH, 2, CMP_HIDDEN, NSA_DH), CMP_HIDDEN ** -0.5),
        'cmp_b2': nrm(ks[22], (DEPTH, 2, NSA_DH), 0.01),
        'w_out': nrm(ks[23], (DEPTH, D_MODEL, D_MODEL), D_MODEL ** -0.5),
        'w_up': nrm(ks[24], (DEPTH, D_MODEL, D_FF), D_MODEL ** -0.5),
        'w_down': nrm(ks[25], (DEPTH, D_FF, D_MODEL), D_FF ** -0.5),
    }


def reference(x_prompt, x_sample, cache_kv, state_win, state_gla, page_table, c_prompt, c_sample,
              norm_mix_pre, norm_mix_post, norm_ffn_pre, norm_ffn_post, w_ada, b_ada, w_in,
              gla_w_gate, gla_b_gate, gla_norm, cmp_pos, cmp_w1, cmp_b1, cmp_w2, cmp_b2,
              w_out, w_up, w_down):
    dec_b = page_table.shape[0]
    bsz = x_prompt.shape[0]
    wbuf = state_win.shape[2]
    y_p, y_s = x_prompt, x_sample
    rows_p, rows_s, win_p, win_s, gla_p, gla_s = [], [], [], [], [], []
    for l in range(DEPTH):
        lw = (norm_mix_pre[l], norm_mix_post[l], norm_ffn_pre[l], norm_ffn_post[l], w_ada[l], b_ada[l],
              w_in[l], gla_w_gate[l], gla_b_gate[l], gla_norm[l], cmp_pos[l], cmp_w1[l], cmp_b1[l],
              cmp_w2[l], cmp_b2[l], w_out[l], w_up[l], w_down[l])
        past_p = jnp.zeros((bsz, 0, 4, NSA_KVH, NSA_DH), x_prompt.dtype)
        win_p0 = jnp.zeros((bsz, 0, 2, NSA_KVH, NSA_DH), x_prompt.dtype)
        gla_p0 = jnp.zeros((bsz, GLA_HEADS, GLA_DK, GLA_DV), jnp.float32)
        y_p, r_p, w_p, s_p = decoder_layer(y_p, c_prompt, past_p, win_p0, gla_p0, wbuf, *lw)
        past_s = cache_kv[l][page_table].reshape(dec_b, -1, 4, NSA_KVH, NSA_DH)
        y_s, r_s, w_s, s_s = decoder_layer(y_s, c_sample, past_s, state_win[l], state_gla[l], wbuf, *lw)
        rows_p.append(r_p)
        rows_s.append(r_s)
        win_p.append(w_p)
        win_s.append(w_s)
        gla_p.append(s_p)
        gla_s.append(s_s)
    return (y_p, y_s, jnp.stack(rows_p), jnp.stack(rows_s), jnp.stack(win_p), jnp.stack(win_s), jnp.stack(gla_p), jnp.stack(gla_s))
```

```python
import functools

import numpy as np
import jax
import jax.numpy as jnp
from jax import lax
from jax.experimental import pallas as pl
from jax.experimental.pallas import tpu as pltpu

F32 = jnp.float32
BF16 = jnp.bfloat16

GLA_HEADS = 4
GLA_DK = 64
GLA_DV = 128
GLA_RANK = 16
GLA_NORMALIZER = 16.0
GLA_CHUNK = 64
NSA_HEADS = 8
NSA_DH = 64
NSA_KVH = 2
NSA_G = NSA_HEADS // NSA_KVH
CMP_STRIDE = 16
CMP_LEN = 32
CMP_HIDDEN = 256
SLC_BLOCK = 64
SLC_TOPN = 16
SEL_FORCE = 1.0e4
WINDOW = 512
EPS = 1e-6
NEG = -1.0e30
PAD_SCORE = -3.0e4
LANES = 128
VMEM_LIMIT = 56 * 1024 * 1024

GLA_QK = GLA_HEADS * GLA_DK
GLA_V = GLA_HEADS * GLA_DV
NSA_Q = NSA_HEADS * NSA_DH
NSA_KV = 6 * NSA_KVH * NSA_DH
NSA_GATE = 3 * NSA_HEADS
KV_LANES = NSA_KVH * NSA_DH


def _sds(shape, dtype=F32):
    return jax.ShapeDtypeStruct(shape, dtype)


def _params(sem):
    return pltpu.CompilerParams(dimension_semantics=sem, vmem_limit_bytes=VMEM_LIMIT)


def _dot(a, b):
    return jnp.dot(a.astype(BF16), b.astype(BF16), preferred_element_type=F32)


def _dot_nt(a, b):
    return lax.dot_general(a.astype(BF16), b.astype(BF16), (((1,), (1,)), ((), ())),
                           preferred_element_type=F32)


def _dot_tn(a, b):
    return lax.dot_general(a.astype(BF16), b.astype(BF16), (((0,), (0,)), ((), ())),
                           preferred_element_type=F32)


def _split3(x):
    p1 = x.astype(BF16)
    r1 = x - p1.astype(F32)
    p2 = r1.astype(BF16)
    p3 = (r1 - p2.astype(F32)).astype(BF16)
    return p1, p2, p3


def _dot_f32(a, b):
    a1, a2, _ = _split3(a)
    b1, b2, _ = _split3(b)
    return (jnp.dot(a1, b1, preferred_element_type=F32)
            + jnp.dot(a1, b2, preferred_element_type=F32)
            + jnp.dot(a2, b1, preferred_element_type=F32))


def _dot_exact_lhs(a_bf16, x):
    out = None
    for p in _split3(x):
        t = jnp.dot(a_bf16, p, preferred_element_type=F32)
        out = t if out is None else out + t
    return out


def _dot_exact_rhs(x, b_bf16):
    out = None
    for p in _split3(x):
        t = jnp.dot(p, b_bf16, preferred_element_type=F32)
        out = t if out is None else out + t
    return out


def _sigmoid(x):
    return 1.0 / (1.0 + jnp.exp(-x))


def _silu(x):
    return x * _sigmoid(x)


def _rms(x, g):
    return x * lax.rsqrt(jnp.mean(x * x, axis=-1, keepdims=True) + EPS) * g


def _ada_body(c_ref, w_ref, b_ref, o_ref):
    o_ref[...] = _dot_f32(_silu(c_ref[...]), w_ref[...]) + b_ref[...]


def _ada(c, w_ada, b_ada):
    n, d = c.shape
    nout = w_ada.shape[1]
    tn = d
    return pl.pallas_call(
        _ada_body,
        grid=(nout // tn,),
        in_specs=[pl.BlockSpec((n, d), lambda j: (0, 0)),
                  pl.BlockSpec((d, tn), lambda j: (0, j)),
                  pl.BlockSpec((1, tn), lambda j: (0, j))],
        out_specs=pl.BlockSpec((n, tn), lambda j: (0, j)),
        out_shape=_sds((n, nout)),
        compiler_params=_params(("arbitrary",)),
        name="ada",
    )(c, w_ada, b_ada.reshape(1, nout))


_C_ZQ = 0
_C_ZK = _C_ZQ + GLA_QK
_C_ZV = _C_ZK + GLA_QK
_C_ZR = _C_ZV + GLA_V
_C_ZQN = _C_ZR + GLA_V
_C_ZKV = _C_ZQN + NSA_Q
_C_MISC = _C_ZKV + NSA_KV
_C_END = _C_MISC + LANES
GATE_COL0 = GLA_RANK


def _inproj_body(x_ref, sc_ref, sh_ref, g_ref, w_ref, wg_ref, bg_ref,
                 qg_ref, kg_ref, vg_ref, rg_ref, qn_ref, zkv_ref, logf_ref, gts_ref):
    h = _rms(x_ref[0], g_ref[...]) * (1.0 + sc_ref[0]) + sh_ref[0]
    hb = h.astype(BF16)

    def proj(a, b):
        return jnp.dot(hb, w_ref[:, a:b], preferred_element_type=F32)

    qg_ref[0] = proj(_C_ZQ, _C_ZK) * (GLA_DK ** -0.5)
    kg_ref[0] = proj(_C_ZK, _C_ZV)
    vg_ref[0] = proj(_C_ZV, _C_ZR)
    rg_ref[0] = _silu(proj(_C_ZR, _C_ZQN))
    qn_ref[0] = proj(_C_ZQN, _C_ZKV) * (NSA_DH ** -0.5)
    zkv_ref[0] = proj(_C_ZKV, _C_MISC)
    zm = proj(_C_MISC, _C_END)
    gts_ref[0] = _sigmoid(zm)
    logit = jnp.dot(zm.astype(BF16), wg_ref[...], preferred_element_type=F32) + bg_ref[...]
    log_sig = jnp.minimum(logit, 0.0) - jnp.log(1.0 + jnp.exp(-jnp.abs(logit)))
    logf_ref[0] = log_sig * (1.0 / GLA_NORMALIZER)


def _mod_spec(mod, tq):
    d = mod.shape[-1]
    if mod.shape[1] == 1:
        return pl.BlockSpec((1, 1, d), lambda b, t: (b, 0, 0))
    return pl.BlockSpec((1, tq, d), lambda b, t: (b, t, 0))


def _inproj(x, sc, sh, gain, w_perm, wg_pad, bg):
    bsz, t, d = x.shape
    tq = min(t, 512)
    assert t % tq == 0
    widths = (GLA_QK, GLA_QK, GLA_V, GLA_V, NSA_Q, NSA_KV, GLA_QK, LANES)
    tok = lambda n: pl.BlockSpec((1, tq, n), lambda b, i: (b, i, 0))
    full = lambda a: pl.BlockSpec(a.shape, lambda b, i: (0,) * a.ndim)
    return pl.pallas_call(
        _inproj_body,
        grid=(bsz, t // tq),
        in_specs=[tok(d), _mod_spec(sc, tq), _mod_spec(sh, tq), full(gain), full(w_perm),
                  full(wg_pad), full(bg)],
        out_specs=[tok(n) for n in widths],
        out_shape=[_sds((bsz, t, n)) for n in widths],
        compiler_params=_params(("arbitrary", "arbitrary")),
        name="inproj",
    )(x, sc, sh, gain, w_perm, wg_pad, bg)


def _gla_body(q_ref, k_ref, v_ref, f_ref, r_ref, s0_ref, gn_ref, o_ref, sfin_ref, s_scr,
              *, rows, nchunks):
    c_len = GLA_CHUNK
    t = pl.program_id(1)

    @pl.when(t == 0)
    def _():
        s_scr[...] = s0_ref[0]

    ri = lax.broadcasted_iota(jnp.int32, (c_len, c_len), 0)
    ci = lax.broadcasted_iota(jnp.int32, (c_len, c_len), 1)
    causal = ci <= ri
    tril = jnp.where(causal, 1.0, 0.0).astype(BF16)
    ones_cv = jnp.ones((c_len, GLA_DV), BF16)
    valid = min(rows, c_len)

    def load(ref, c):
        x = ref[0, c * c_len:c * c_len + valid, :]
        if valid < c_len:
            x = jnp.concatenate([x, jnp.zeros((c_len - valid, x.shape[1]), F32)], axis=0)
        return x

    for c in range(nchunks):
        q, k, v, g, r = (load(ref, c) for ref in (q_ref, k_ref, v_ref, f_ref, r_ref))
        gparts = _split3(g)
        b = None
        for gp in gparts:
            term = jnp.dot(tril, gp, preferred_element_type=F32)
            b = term if b is None else b + term
        b_last = b[c_len - 1:c_len, :]
        b_mid = b[c_len // 2 - 1:c_len // 2, :]
        q_inter = q * jnp.exp(b)
        q_intra = q * jnp.exp(b - b_mid)
        k_intra = k * jnp.exp(b_mid - b)
        k_out = k * jnp.exp(b_last - b)
        for h in range(GLA_HEADS):
            ks = slice(h * GLA_DK, (h + 1) * GLA_DK)
            vs = slice(h * GLA_DV, (h + 1) * GLA_DV)
            a = jnp.where(causal, _dot_nt(q_intra[:, ks], k_intra[:, ks]), 0.0)
            s_old = s_scr[h]
            o = _dot(a, v[:, vs]) + _dot(q_inter[:, ks], s_old)
            tot = None
            for gp in gparts:
                term = lax.dot_general(gp[:, ks], ones_cv, (((0,), (0,)), ((), ())),
                                       preferred_element_type=F32)
                tot = term if tot is None else tot + term
            s_scr[h] = s_old * jnp.exp(tot) + _dot_tn(k_out[:, ks], v[:, vs])
            on = _rms(o, gn_ref[...]) * r[:, vs]
            o_ref[0, c * c_len:c * c_len + valid, vs] = on[:valid]

    @pl.when(t == pl.num_programs(1) - 1)
    def _():
        sfin_ref[0] = s_scr[...]


def _gla(qg, kg, vg, logf, rg, s0, gnorm):
    bsz, t, _ = qg.shape
    tt = min(t, 4 * GLA_CHUNK)
    assert t % tt == 0 and (tt % GLA_CHUNK == 0 or tt < GLA_CHUNK)
    nchunks = max(tt // GLA_CHUNK, 1)
    tok = lambda n: pl.BlockSpec((1, tt, n), lambda b, i: (b, i, 0))
    st = pl.BlockSpec((1, GLA_HEADS, GLA_DK, GLA_DV), lambda b, i: (b, 0, 0, 0))
    return pl.pallas_call(
        functools.partial(_gla_body, rows=tt, nchunks=nchunks),
        grid=(bsz, t // tt),
        in_specs=[tok(GLA_QK), tok(GLA_QK), tok(GLA_V), tok(GLA_QK), tok(GLA_V), st,
                  pl.BlockSpec((1, GLA_DV), lambda b, i: (0, 0))],
        out_specs=[tok(GLA_V), st],
        out_shape=[_sds((bsz, t, GLA_V)), _sds((bsz, GLA_HEADS, GLA_DK, GLA_DV))],
        scratch_shapes=[pltpu.VMEM((GLA_HEADS, GLA_DK, GLA_DV), F32)],
        compiler_params=_params(("arbitrary", "arbitrary")),
        name="gla",
    )(qg, kg, vg, logf, rg, s0, gnorm)


def _posbias_body(p_ref, w_ref, o_ref):
    o_ref[0, 0] = _dot_f32(p_ref[0, 0], w_ref[0, 0])


def _posbias(cmp_pos, cmp_w1):
    kdim = CMP_STRIDE * NSA_DH
    pos = jnp.broadcast_to(cmp_pos.reshape(2, 2, 1, kdim), (2, 2, 8, kdim))
    w = cmp_w1.reshape(2, 2, kdim, CMP_HIDDEN)
    return pl.pallas_call(
        _posbias_body,
        grid=(2, 2),
        in_specs=[pl.BlockSpec((1, 1, 8, kdim), lambda a, b: (a, b, 0, 0)),
                  pl.BlockSpec((1, 1, kdim, CMP_HIDDEN), lambda a, b: (a, b, 0, 0))],
        out_specs=pl.BlockSpec((1, 1, 8, CMP_HIDDEN), lambda a, b: (a, b, 0, 0)),
        out_shape=_sds((2, 2, 8, CMP_HIDDEN)),
        compiler_params=_params(("arbitrary", "arbitrary")),
        name="posbias",
    )(pos, w)


def _compress_body(*refs, nblk, prefetch):
    if prefetch:
        refs = refs[1:]
    row_refs = refs[:nblk]
    nxt_ref, w1_ref, pb_ref, b1_ref, w2_ref, b2_ref, o_ref = refs[nblk:]
    per = row_refs[0].shape[1] // CMP_STRIDE
    mc = nblk * per
    acc = [jnp.zeros((mc + 8, 2 * CMP_HIDDEN), F32) for _ in range(NSA_KVH)]
    for j in range(CMP_STRIDE):
        xs = [r[0, pl.ds(j, per, stride=CMP_STRIDE), :] for r in row_refs]
        xs.append(nxt_ref[0, pl.ds(j, 8, stride=CMP_STRIDE), :])
        xj = jnp.concatenate(xs, axis=0).astype(BF16)
        for h in range(NSA_KVH):
            acc[h] = acc[h] + jnp.dot(xj[:, h * NSA_DH:(h + 1) * NSA_DH], w1_ref[0, j],
                                      preferred_element_type=F32)
    bias = pb_ref[0, 0, 0:1, :] + pb_ref[0, 1, 0:1, :] + b1_ref[0]
    outs = []
    for h in range(NSA_KVH):
        z = acc[h]
        pre = z[0:mc, 0:CMP_HIDDEN] + z[1:mc + 1, CMP_HIDDEN:2 * CMP_HIDDEN] + bias
        outs.append(_dot(_silu(pre), w2_ref[0]) + b2_ref[0])
    o_ref[0, 0] = jnp.concatenate(outs, axis=1)


def _compress_weights(cmp_w1, cmp_w2):
    w1 = jnp.concatenate([cmp_w1[:, :CMP_STRIDE], cmp_w1[:, CMP_STRIDE:]], axis=-1)
    return w1.astype(BF16), cmp_w2.astype(BF16)


def _compress_common_specs(nargs_extra):
    def wspec(shape):
        nd = len(shape)
        return pl.BlockSpec((1,) + shape[1:], lambda cc, b, t, *_: (cc,) + (0,) * (nd - 1))
    return wspec


def _compress_prompt(zkv, w1, posb, b1, w2, b2):
    bsz, t, _ = zkv.shape
    rows = min(t, 1024)
    assert t % rows == 0 and rows % LANES == 0
    nch = t // CMP_STRIDE
    ntile = t // rows
    last = t // LANES - 1
    wspec = _compress_common_specs(0)
    return pl.pallas_call(
        functools.partial(_compress_body, nblk=1, prefetch=False),
        grid=(2, bsz, ntile),
        in_specs=[pl.BlockSpec((1, rows, KV_LANES), lambda cc, b, i: (b, i, cc)),
                  pl.BlockSpec((1, LANES, KV_LANES),
                               lambda cc, b, i: (b, jnp.minimum((i + 1) * (rows // LANES), last), cc)),
                  wspec(w1.shape), wspec(posb.shape), wspec(b1.shape), wspec(w2.shape),
                  wspec(b2.shape)],
        out_specs=pl.BlockSpec((1, 1, rows // CMP_STRIDE, KV_LANES), lambda cc, b, i: (cc, b, i, 0)),
        out_shape=_sds((2, bsz, nch, KV_LANES)),
        compiler_params=_params(("arbitrary", "arbitrary", "arbitrary")),
        name="compress_prompt",
    )(zkv, zkv, w1, posb, b1, w2, b2)


def _compress_paged(cache, page_table, w1, posb, b1, w2, b2):
    bsz, npages = page_table.shape
    page = cache.shape[1]
    assert page == LANES
    ppt = min(npages, 8)
    assert npages % ppt == 0
    ntile = npages // ppt
    nch = npages * page // CMP_STRIDE
    wspec = _compress_common_specs(1)

    def page_spec(p):
        return pl.BlockSpec((1, page, KV_LANES), lambda cc, b, i, pt: (pt[b, i * ppt + p], 0, cc))

    nxt = pl.BlockSpec((1, page, KV_LANES),
                       lambda cc, b, i, pt: (pt[b, jnp.minimum((i + 1) * ppt, npages - 1)], 0, cc))
    grid_spec = pltpu.PrefetchScalarGridSpec(
        num_scalar_prefetch=1,
        grid=(2, bsz, ntile),
        in_specs=[page_spec(p) for p in range(ppt)] + [nxt, wspec(w1.shape), wspec(posb.shape),
                                                       wspec(b1.shape), wspec(w2.shape),
                                                       wspec(b2.shape)],
        out_specs=pl.BlockSpec((1, 1, ppt * page // CMP_STRIDE, KV_LANES),
                               lambda cc, b, i, pt: (cc, b, i, 0)),
    )
    return pl.pallas_call(
        functools.partial(_compress_body, nblk=ppt, prefetch=True),
        grid_spec=grid_spec,
        out_shape=_sds((2, bsz, nch, KV_LANES)),
        compiler_params=_params(("arbitrary", "arbitrary", "arbitrary")),
        name="compress_paged",
    )(page_table, *([cache] * (ppt + 1)), w1, posb, b1, w2, b2)


def _cmp_to_slc_map(nc, ns):
    start = np.arange(nc) * CMP_STRIDE
    bs = np.arange(ns) * SLC_BLOCK
    ov = (np.minimum(start[:, None] + CMP_LEN, bs[None, :] + SLC_BLOCK)
          - np.maximum(start[:, None], bs[None, :]))
    return (np.clip(ov, 0, None) / CMP_LEN).astype(np.float32)


def _select_top(score_t, ns, n_sel):
    nrow, nq = score_t.shape
    groups = [score_t[8 * v:8 * v + 8, :] for v in range(nrow // 8)]
    ranks = [jnp.zeros((8, nq), F32) for _ in groups]
    sub = lax.broadcasted_iota(jnp.int32, (8, nq), 0)
    for i in range(ns):
        row = jnp.broadcast_to(score_t[i:i + 1, :], (8, nq))
        for v, grp in enumerate(groups):
            if i < 8 * v:
                ranks[v] = jnp.where(row >= grp, ranks[v] + 1.0, ranks[v])
            elif i >= 8 * v + 8:
                ranks[v] = jnp.where(row > grp, ranks[v] + 1.0, ranks[v])
            else:
                later = jnp.where(row >= grp, 1.0, 0.0)
                earlier = jnp.where(row > grp, 1.0, 0.0)
                ranks[v] = ranks[v] + jnp.where(sub > (i - 8 * v), later, earlier)
    rank = jnp.concatenate(ranks, axis=0)
    return jnp.where(rank < float(n_sel), 1.0, 0.0)


def _nsa_prompt_body(q_ref, kc_ref, vc_ref, ks_ref, vs_ref, kw_ref, vw_ref, g_ref,
                     slope_ref, dq_ref, slope_c_ref, dc_ref, mmap_ref,
                     o_ref, acc_scr, m_scr, l_scr, *, tq, ns, n_sel):
    i = pl.program_id(1)
    t0 = i * tq
    tk = tq
    m_rows = NSA_G * tq
    lane = lax.broadcasted_iota(jnp.int32, (tq, 2 * NSA_DH), 1)
    dq = dq_ref[...]
    gates = g_ref[0]
    out_cols = []
    for h in range(NSA_KVH):
        qs = []
        for g in range(NSA_G):
            c0 = (h * NSA_G + g) * NSA_DH
            qg = q_ref[0, :, c0:c0 + NSA_DH]
            qq = jnp.concatenate([qg, qg], axis=1)
            qs.append(jnp.where((lane >= h * NSA_DH) & (lane < (h + 1) * NSA_DH), qq, 0.0))
        q = jnp.concatenate(qs, axis=0).astype(BF16)
        slope = slope_ref[h]

        d_c = dc_ref[...] + t0.astype(F32)
        s_c = _dot_nt(q, kc_ref[0, 0])
        s_c = jnp.where(d_c >= 0.0, s_c - slope_c_ref[h] * d_c, NEG)
        mx = jnp.max(s_c, axis=-1, keepdims=True)
        p = jnp.where(s_c > 0.5 * NEG, jnp.exp(s_c - mx), 0.0)
        p_c = p * (1.0 / jnp.maximum(jnp.sum(p, axis=-1, keepdims=True), 1e-30))
        o_c = _dot(p_c, vc_ref[0, 0])

        p_sum = p_c[0:tq]
        for g in range(1, NSA_G):
            p_sum = p_sum + p_c[g * tq:(g + 1) * tq]
        imp = _dot_exact_rhs(p_sum, mmap_ref[...])
        blk = lax.broadcasted_iota(jnp.int32, (tq, ns), 1)
        cur = (t0 + lax.broadcasted_iota(jnp.int32, (tq, ns), 0)) // SLC_BLOCK
        forced = (blk == 0) | (blk == cur) | (blk == cur - 1)
        score = jnp.where(blk <= cur, jnp.where(forced, SEL_FORCE, imp), -SEL_FORCE)
        if ns < LANES:
            score = jnp.concatenate([score, jnp.full((tq, LANES - ns), PAD_SCORE, F32)], axis=1)
        ns8 = -(-ns // 8) * 8
        sel_t = _select_top(score.T[0:ns8], ns, n_sel).astype(BF16)

        def flash_tile(j, use_sel):
            k0 = j * tk
            kt = (ks_ref if use_sel else kw_ref)[0, pl.ds(pl.multiple_of(k0, tk), tk), :]
            vt = (vs_ref if use_sel else vw_ref)[0, pl.ds(pl.multiple_of(k0, tk), tk), :]
            s = _dot_nt(q, kt)
            dist = dq + (t0 - k0).astype(F32)
            s = s - slope * dist
            if use_sel:
                brow = lax.broadcasted_iota(jnp.int32, (ns8, tk), 0)
                bkey = (k0 + lax.broadcasted_iota(jnp.int32, (ns8, tk), 1)) // SLC_BLOCK
                expand = jnp.where(brow == bkey, 1.0, 0.0).astype(BF16)
                mask = lax.dot_general(sel_t, expand, (((0,), (0,)), ((), ())),
                                       preferred_element_type=F32)
                mask = jnp.concatenate([mask] * NSA_G, axis=0)
                s = jnp.where(mask > 0.5, jnp.where(dist >= 0.0, s, NEG), NEG)
            else:
                s = jnp.where(dist >= 0.0, jnp.where(dist <= float(WINDOW), s, NEG), NEG)
            m_old = m_scr[...]
            m_new = jnp.maximum(m_old, jnp.max(s, axis=-1, keepdims=True))
            alpha = jnp.exp(m_old - m_new)
            p = jnp.where(s > 0.5 * NEG, jnp.exp(s - m_new), 0.0)
            l_scr[...] = alpha * l_scr[...] + jnp.sum(p, axis=-1, keepdims=True)
            acc_scr[...] = alpha * acc_scr[...] + _dot(p, vt)
            m_scr[...] = m_new

        def run_branch(lo, use_sel):
            m_scr[...] = jnp.full((m_rows, 1), NEG, F32)
            l_scr[...] = jnp.zeros((m_rows, 1), F32)
            acc_scr[...] = jnp.zeros((m_rows, KV_LANES), F32)

            def body(j, carry):
                flash_tile(j, use_sel)
                return carry

            lax.fori_loop(lo, i + 1, body, 0)
            return acc_scr[...] * (1.0 / l_scr[...])

        o_s = run_branch(0, True)
        o_w = run_branch(jnp.maximum(i - WINDOW // tk, 0), False)

        for g in range(NSA_G):
            col = GATE_COL0 + (h * NSA_G + g) * 3
            rs = slice(g * tq, (g + 1) * tq)
            hs = slice(h * NSA_DH, (h + 1) * NSA_DH)
            out_cols.append(gates[:, col:col + 1] * o_c[rs, hs]
                            + gates[:, col + 1:col + 2] * o_s[rs, hs]
                            + gates[:, col + 2:col + 3] * o_w[rs, hs])
    o_ref[0] = jnp.concatenate(out_cols, axis=1)


def _alibi_slopes():
    h = np.arange(1, NSA_HEADS + 1, dtype=np.float32)
    return np.exp2(-8.0 * h / NSA_HEADS).astype(np.float32).reshape(NSA_KVH, NSA_G)


def _nsa_prompt(qn, kcvc, zkv, gts):
    bsz, t, _ = qn.shape
    tq = min(t, 256)
    assert t % tq == 0 and WINDOW % tq == 0 and tq % SLC_BLOCK == 0
    nch = kcvc.shape[2]
    ns = -(-t // SLC_BLOCK)
    n_sel = min(SLC_TOPN, ns)
    m_rows = NSA_G * tq
    slopes = _alibi_slopes()
    qi = np.tile(np.arange(tq, dtype=np.float32), NSA_G)
    slope_rows = np.repeat(slopes, tq, axis=1)
    slope_mat = np.ascontiguousarray(np.broadcast_to(slope_rows[:, :, None], (NSA_KVH, m_rows, tq)))
    dq_mat = qi[:, None] - np.arange(tq, dtype=np.float32)[None, :]
    slope_c = np.ascontiguousarray(np.broadcast_to(slope_rows[:, :, None], (NSA_KVH, m_rows, nch)))
    cmp_end = np.arange(nch, dtype=np.float32) * CMP_STRIDE + (CMP_LEN - 1)
    dc_mat = qi[:, None] - cmp_end[None, :]
    mmap = jnp.asarray(_cmp_to_slc_map(nch, ns), BF16)

    kv = lambda c: pl.BlockSpec((1, t, KV_LANES), lambda b, i: (b, 0, c))
    cst = lambda a: pl.BlockSpec(a.shape, lambda b, i: (0,) * a.ndim)
    consts = [jnp.asarray(slope_mat), jnp.asarray(dq_mat), jnp.asarray(slope_c),
              jnp.asarray(dc_mat), mmap]
    return pl.pallas_call(
        functools.partial(_nsa_prompt_body, tq=tq, ns=ns, n_sel=n_sel),
        grid=(bsz, t // tq),
        in_specs=[pl.BlockSpec((1, tq, NSA_Q), lambda b, i: (b, i, 0)),
                  pl.BlockSpec((1, 1, nch, KV_LANES), lambda b, i: (0, b, 0, 0)),
                  pl.BlockSpec((1, 1, nch, KV_LANES), lambda b, i: (1, b, 0, 0)),
                  kv(2), kv(3), kv(4), kv(5),
                  pl.BlockSpec((1, tq, LANES), lambda b, i: (b, i, 0))] + [cst(a) for a in consts],
        out_specs=pl.BlockSpec((1, tq, NSA_Q), lambda b, i: (b, i, 0)),
        out_shape=_sds((bsz, t, NSA_Q)),
        scratch_shapes=[pltpu.VMEM((m_rows, KV_LANES), F32), pltpu.VMEM((m_rows, 1), F32),
                        pltpu.VMEM((m_rows, 1), F32)],
        compiler_params=_params(("arbitrary", "arbitrary")),
        name="nsa_prompt",
    )(qn, kcvc, kcvc, zkv, zkv, zkv, zkv, gts, *consts)


def _nsa_sample_body(pt_ref, q_ref, kc_ref, vc_ref, rows_ref, sw_ref, g_ref,
                     slope_ref, pos_ref, cur_ref, dc_ref, gsum_ref, mmap_ref, cache_ref,
                     o_ref, wout_ref, kvbuf, sem, *, past, t, npages, ns, n_sel):
    b = pl.program_id(0)
    nb = pl.num_programs(0)
    page = past // npages
    m_cols = NSA_HEADS * t
    half = m_cols // NSA_KVH

    def page_copy(bb, slot, p):
        return pltpu.make_async_copy(
            cache_ref.at[pt_ref[bb, p], :, pl.ds(2 * KV_LANES, 2 * KV_LANES)],
            kvbuf.at[slot, pl.ds(p * page, page), :], sem.at[slot])

    @pl.when(b == 0)
    def _():
        for p in range(npages):
            page_copy(0, 0, p).start()

    @pl.when(b + 1 < nb)
    def _():
        for p in range(npages):
            page_copy(b + 1, (b + 1) % 2, p).start()

    pieces = []
    for hh in range(NSA_HEADS):
        qg = q_ref[0, :, hh * NSA_DH:(hh + 1) * NSA_DH]
        z = jnp.zeros_like(qg)
        pieces.append(jnp.concatenate([qg, z] if hh < NSA_G else [z, qg], axis=1))
    qbd = jnp.concatenate(pieces, axis=0).astype(BF16)
    slope = slope_ref[...]
    pos = pos_ref[...]

    def softmax_t(s):
        mx = jnp.max(s, axis=0, keepdims=True)
        p = jnp.where(s > 0.5 * NEG, jnp.exp(s - mx), 0.0)
        return p * (1.0 / jnp.maximum(jnp.sum(p, axis=0, keepdims=True), 1e-30))

    d_c = dc_ref[...]
    s_c = _dot_nt(kc_ref[0, 0], qbd)
    p_c = softmax_t(jnp.where(d_c >= 0.0, s_c - slope * d_c, NEG))
    o_c = _dot_tn(p_c, vc_ref[0, 0])

    imp_g = _dot_exact_lhs(mmap_ref[...], p_c)
    imp = _dot_exact_rhs(imp_g, gsum_ref[...])
    ns8 = imp.shape[0]
    blk = lax.broadcasted_iota(jnp.int32, (ns8, m_cols), 0)
    cur = cur_ref[...]
    forced = (blk == 0) | (blk == cur) | (blk == cur - 1)
    score = jnp.where(blk <= cur, jnp.where(forced, SEL_FORCE, imp), -SEL_FORCE)
    score = jnp.where(blk < ns, score, PAD_SCORE)
    sel_t = _select_top(score, ns, n_sel)

    w_all = jnp.concatenate([sw_ref[0], rows_ref[0, :, 4 * KV_LANES:6 * KV_LANES]], axis=0)
    wlen = w_all.shape[0]
    wout_ref[0] = w_all[wlen - wout_ref.shape[1]:, :]
    s_w = _dot_nt(w_all[:, 0:KV_LANES], qbd)
    kpos_w = (lax.broadcasted_iota(jnp.int32, (wlen, m_cols), 0)
              + (past + t - wlen)).astype(F32)
    dist_w = pos - kpos_w
    s_w = jnp.where(dist_w >= 0.0, jnp.where(dist_w <= float(WINDOW), s_w - slope * dist_w, NEG), NEG)
    o_w = _dot_tn(softmax_t(s_w), w_all[:, KV_LANES:2 * KV_LANES])

    slot = b % 2
    for p in range(npages):
        page_copy(b, slot, p).wait()
    new_k = rows_ref[0, :, 2 * KV_LANES:3 * KV_LANES]
    new_v = rows_ref[0, :, 3 * KV_LANES:4 * KV_LANES]
    k_all = jnp.concatenate([kvbuf[slot, :, 0:KV_LANES], new_k], axis=0)
    v_all = jnp.concatenate([kvbuf[slot, :, KV_LANES:2 * KV_LANES], new_v], axis=0)
    nkeys = past + t
    nfull = past // SLC_BLOCK
    s_s = _dot_nt(k_all, qbd)
    dist = pos - lax.broadcasted_iota(jnp.int32, (nkeys, m_cols), 0).astype(F32)
    sel_rows = jnp.broadcast_to(sel_t[0:nfull][:, None, :], (nfull, SLC_BLOCK, m_cols))
    sel_keys = jnp.concatenate(
        [sel_rows.reshape(past, m_cols), jnp.broadcast_to(sel_t[nfull:nfull + 1], (t, m_cols))], axis=0)
    s_s = jnp.where(sel_keys > 0.5, jnp.where(dist >= 0.0, s_s - slope * dist, NEG), NEG)
    o_s = _dot_tn(softmax_t(s_s), v_all)

    def own(x):
        return jnp.concatenate([x[0:half, 0:NSA_DH], x[half:, NSA_DH:2 * NSA_DH]], axis=0)

    gates = g_ref[0]

    def gate_col(br):
        return jnp.concatenate(
            [gates[:, GATE_COL0 + hh * 3 + br:GATE_COL0 + hh * 3 + br + 1] for hh in range(NSA_HEADS)],
            axis=0)

    mix = gate_col(0) * own(o_c) + gate_col(1) * own(o_s) + gate_col(2) * own(o_w)
    o_ref[0] = jnp.concatenate([mix[hh * t:(hh + 1) * t, :] for hh in range(NSA_HEADS)], axis=1)


def _nsa_sample(qn, kcvc, rows, state_win, gts, cache, page_table):
    bsz, t, _ = qn.shape
    npages = page_table.shape[1]
    page = cache.shape[1]
    past = npages * page
    pw = state_win.shape[1]
    assert past % SLC_BLOCK == 0 and t <= SLC_BLOCK and t % 8 == 0 and pw + t >= WINDOW
    nch = kcvc.shape[2]
    ns = -(-(past + t) // SLC_BLOCK)
    ns8 = -(-ns // 8) * 8
    n_sel = min(SLC_TOPN, ns)
    m_cols = NSA_HEADS * t
    slopes = _alibi_slopes().reshape(-1)
    slope_row = np.repeat(slopes, t)[None, :].astype(np.float32)
    pos_i = np.tile(past + np.arange(t), NSA_HEADS)[None, :]
    cmp_end = np.arange(nch, dtype=np.float32) * CMP_STRIDE + (CMP_LEN - 1)
    dc_mat = pos_i.astype(np.float32) - cmp_end[:, None]
    col_kvh = np.arange(m_cols) // (NSA_G * t)
    col_tok = np.arange(m_cols) % t
    gsum = ((col_kvh[:, None] == col_kvh[None, :]) & (col_tok[:, None] == col_tok[None, :]))
    mmap_t = np.zeros((ns8, nch), np.float32)
    mmap_t[:ns] = _cmp_to_slc_map(nch, ns).T
    consts = [jnp.asarray(slope_row), jnp.asarray(pos_i.astype(np.float32)),
              jnp.asarray((pos_i // SLC_BLOCK).astype(np.int32)), jnp.asarray(dc_mat),
              jnp.asarray(gsum.astype(np.float32), BF16), jnp.asarray(mmap_t, BF16)]

    cst = lambda a: pl.BlockSpec(a.shape, lambda b, pt: (0,) * a.ndim)
    grid_spec = pltpu.PrefetchScalarGridSpec(
        num_scalar_prefetch=1,
        grid=(bsz,),
        in_specs=[pl.BlockSpec((1, t, NSA_Q), lambda b, pt: (b, 0, 0)),
                  pl.BlockSpec((1, 1, nch, KV_LANES), lambda b, pt: (0, b, 0, 0)),
                  pl.BlockSpec((1, 1, nch, KV_LANES), lambda b, pt: (1, b, 0, 0)),
                  pl.BlockSpec((1, t, NSA_KV), lambda b, pt: (b, 0, 0)),
                  pl.BlockSpec((1, pw, 2 * KV_LANES), lambda b, pt: (b, 0, 0)),
                  pl.BlockSpec((1, t, LANES), lambda b, pt: (b, 0, 0))]
        + [cst(a) for a in consts] + [pl.BlockSpec(memory_space=pl.ANY)],
        out_specs=[pl.BlockSpec((1, t, NSA_Q), lambda b, pt: (b, 0, 0)),
                   pl.BlockSpec((1, WINDOW, 2 * KV_LANES), lambda b, pt: (b, 0, 0))],
        scratch_shapes=[pltpu.VMEM((2, past, 2 * KV_LANES), F32), pltpu.SemaphoreType.DMA((2,))],
    )
    return pl.pallas_call(
        functools.partial(_nsa_sample_body, past=past, t=t, npages=npages, ns=ns, n_sel=n_sel),
        grid_spec=grid_spec,
        out_shape=[_sds((bsz, t, NSA_Q)), _sds((bsz, WINDOW, 2 * KV_LANES))],
        compiler_params=_params(("arbitrary",)),
        name="nsa_sample",
    )(page_table, qn, kcvc, kcvc, rows, state_win, gts, *consts, cache)


def _ffn_body(x_ref, og_ref, on_ref, gtm_ref, scf_ref, shf_ref, gtf_ref,
              nmp_ref, nfp_ref, nfo_ref, wo_ref, wu_ref, wd_ref, y_ref, *, ff_chunk):
    mix = (jnp.dot(og_ref[0].astype(BF16), wo_ref[0:GLA_V, :], preferred_element_type=F32)
           + jnp.dot(on_ref[0].astype(BF16), wo_ref[GLA_V:GLA_V + NSA_Q, :],
                     preferred_element_type=F32))
    x1 = x_ref[0] + gtm_ref[0] * _rms(mix, nmp_ref[...])
    hb = (_rms(x1, nfp_ref[...]) * (1.0 + scf_ref[0]) + shf_ref[0]).astype(BF16)
    f = None
    for c in range(wu_ref.shape[1] // ff_chunk):
        cs = slice(c * ff_chunk, (c + 1) * ff_chunk)
        u = jnp.maximum(jnp.dot(hb, wu_ref[:, cs], preferred_element_type=F32), 0.0)
        term = jnp.dot((u * u).astype(BF16), wd_ref[cs, :], preferred_element_type=F32)
        f = term if f is None else f + term
    y_ref[0] = x1 + gtf_ref[0] * _rms(f, nfo_ref[...])


def _ffn(x, og, on, gtm, scf, shf, gtf, nmp, nfp, nfo, wo, wu, wd):
    bsz, t, d = x.shape
    tq = min(t, 512)
    assert t % tq == 0
    tok = lambda n: pl.BlockSpec((1, tq, n), lambda b, i: (b, i, 0))
    full = lambda a: pl.BlockSpec(a.shape, lambda b, i: (0,) * a.ndim,
                                  pipeline_mode=pl.Buffered(1))
    vec = lambda a: pl.BlockSpec(a.shape, lambda b, i: (0,) * a.ndim)
    return pl.pallas_call(
        functools.partial(_ffn_body, ff_chunk=min(wu.shape[1], 1024)),
        grid=(bsz, t // tq),
        in_specs=[tok(d), tok(GLA_V), tok(NSA_Q), _mod_spec(gtm, tq), _mod_spec(scf, tq),
                  _mod_spec(shf, tq), _mod_spec(gtf, tq), vec(nmp), vec(nfp), vec(nfo),
                  full(wo), full(wu), full(wd)],
        out_specs=tok(d),
        out_shape=_sds((bsz, t, d)),
        compiler_params=_params(("arbitrary", "arbitrary")),
        name="ffn",
    )(x, og, on, gtm, scf, shf, gtf, nmp, nfp, nfo, wo, wu, wd)


def _permute_w_in(w_in):
    o_zq = 0
    o_zk = o_zq + GLA_QK
    o_zv = o_zk + GLA_QK
    o_za = o_zv + GLA_V
    o_zr = o_za + GLA_RANK
    o_zqn = o_zr + GLA_V
    o_zkv = o_zqn + NSA_Q
    o_zg = o_zkv + NSA_KV
    o_end = o_zg + NSA_GATE
    d = w_in.shape[0]
    pad = jnp.zeros((d, LANES - GLA_RANK - NSA_GATE), w_in.dtype)
    cols = [w_in[:, o_zq:o_za], w_in[:, o_zr:o_zg], w_in[:, o_za:o_zr], w_in[:, o_zg:o_end], pad]
    return jnp.concatenate(cols, axis=1).astype(BF16)


def _layer_weights(l, norm_mix_pre, norm_mix_post, norm_ffn_pre, norm_ffn_post, w_in,
                   gla_w_gate, gla_b_gate, gla_norm, cmp_pos, cmp_w1, cmp_b1, cmp_w2, cmp_b2,
                   w_out, w_up, w_down):
    wg_pad = jnp.zeros((LANES, GLA_QK), F32).at[:GLA_RANK].set(gla_w_gate[l]).astype(BF16)
    w1, w2 = _compress_weights(cmp_w1[l], cmp_w2[l])
    return dict(
        nmpre=norm_mix_pre[l][None, :], nmpost=norm_mix_post[l][None, :],
        nfpre=norm_ffn_pre[l][None, :], nfpost=norm_ffn_post[l][None, :],
        w_perm=_permute_w_in(w_in[l]), wg_pad=wg_pad, bg=gla_b_gate[l][None, :],
        gnorm=gla_norm[l][None, :], w1=w1, w2=w2, posb=_posbias(cmp_pos[l], cmp_w1[l]),
        b1=cmp_b1[l][:, None, :], b2=cmp_b2[l][:, None, :],
        wo=w_out[l].astype(BF16), wu=w_up[l].astype(BF16), wd=w_down[l].astype(BF16))


def _split_ada(ada, rows_per_batch):
    parts = jnp.split(ada, 6, axis=-1)
    if rows_per_batch is None:
        return [p[:, None, :] for p in parts]
    return [jnp.repeat(p, rows_per_batch, axis=0)[None] for p in parts]


def _prompt_layer(x, ada, w, wbuf):
    bsz, t, d = x.shape
    sh_m, sc_m, gt_m, sh_f, sc_f, gt_f = _split_ada(ada, None)
    qg, kg, vg, rg, qn, zkv, logf, gts = _inproj(x, sc_m, sh_m, w["nmpre"], w["w_perm"],
                                                 w["wg_pad"], w["bg"])
    s0 = jnp.zeros((bsz, GLA_HEADS, GLA_DK, GLA_DV), F32)
    og, s_fin = _gla(qg, kg, vg, logf, rg, s0, w["gnorm"])
    kcvc = _compress_prompt(zkv, w["w1"], w["posb"], w["b1"], w["w2"], w["b2"])
    on = _nsa_prompt(qn, kcvc, zkv, gts)
    y = _ffn(x, og, on, gt_m, sc_f, sh_f, gt_f, w["nmpost"], w["nfpre"], w["nfpost"],
             w["wo"], w["wu"], w["wd"])
    rows = zkv[:, :, :4 * KV_LANES].reshape(bsz, t, 4, NSA_KVH, NSA_DH)
    win = zkv[:, :, 4 * KV_LANES:].reshape(bsz, t, 2, NSA_KVH, NSA_DH)
    if t < wbuf:
        win = jnp.pad(win, ((0, 0), (wbuf - t, 0), (0, 0), (0, 0), (0, 0)))
    return y, rows, win[:, -wbuf:], s_fin


def _sample_layer(x, ada, cache, page_table, state_win, state_gla, w):
    bsz, t, d = x.shape
    n = bsz * t
    sh_m, sc_m, gt_m, sh_f, sc_f, gt_f = _split_ada(ada, t)
    outs = _inproj(x.reshape(1, n, d), sc_m, sh_m, w["nmpre"], w["w_perm"], w["wg_pad"], w["bg"])
    qg, kg, vg, rg, qn, zkv, logf, gts = [o.reshape(bsz, t, o.shape[-1]) for o in outs]
    og, s_fin = _gla(qg, kg, vg, logf, rg, state_gla, w["gnorm"])
    cache2 = cache.reshape(cache.shape[0], cache.shape[1], 4 * KV_LANES)
    kcvc = _compress_paged(cache2, page_table, w["w1"], w["posb"], w["b1"], w["w2"], w["b2"])
    pw = state_win.shape[1]
    on, win = _nsa_sample(qn, kcvc, zkv, state_win.reshape(bsz, pw, 2 * KV_LANES), gts,
                          cache2, page_table)
    y = _ffn(x.reshape(1, n, d), og.reshape(1, n, GLA_V), on.reshape(1, n, NSA_Q),
             gt_m, sc_f, sh_f, gt_f, w["nmpost"], w["nfpre"], w["nfpost"],
             w["wo"], w["wu"], w["wd"])
    rows = zkv[:, :, :4 * KV_LANES].reshape(bsz, t, 4, NSA_KVH, NSA_DH)
    return (y.reshape(bsz, t, d), rows, win.reshape(bsz, WINDOW, 2, NSA_KVH, NSA_DH), s_fin)


def kernel(x_prompt, x_sample, cache_kv, state_win, state_gla, page_table, c_prompt, c_sample,
           norm_mix_pre, norm_mix_post, norm_ffn_pre, norm_ffn_post, w_ada, b_ada, w_in,
           gla_w_gate, gla_b_gate, gla_norm, cmp_pos, cmp_w1, cmp_b1, cmp_w2, cmp_b2,
           w_out, w_up, w_down):
    depth = w_in.shape[0]
    bsz = x_prompt.shape[0]
    wbuf = state_win.shape[2]
    assert wbuf == WINDOW
    c_all = jnp.concatenate([c_prompt, c_sample], axis=0)
    y_p, y_s = x_prompt, x_sample
    outs = [[] for _ in range(6)]
    for l in range(depth):
        w = _layer_weights(l, norm_mix_pre, norm_mix_post, norm_ffn_pre, norm_ffn_post, w_in,
                           gla_w_gate, gla_b_gate, gla_norm, cmp_pos, cmp_w1, cmp_b1, cmp_w2,
                           cmp_b2, w_out, w_up, w_down)
        ada = _ada(c_all, w_ada[l], b_ada[l])
        y_p, r_p, w_p, s_p = _prompt_layer(y_p, ada[:bsz], w, wbuf)
        y_s, r_s, w_s, s_s = _sample_layer(y_s, ada[bsz:], cache_kv[l], page_table,
                                           state_win[l], state_gla[l], w)
        for lst, val in zip(outs, (r_p, r_s, w_p, w_s, s_p, s_s)):
            lst.append(val)
    return (y_p, y_s) + tuple(jnp.stack(o) for o in outs)
```

```python
import functools

import numpy as np
import jax
import jax.numpy as jnp
from jax import lax
from jax.experimental import pallas as pl
from jax.experimental.pallas import tpu as pltpu

F32 = jnp.float32
BF16 = jnp.bfloat16

GLA_HEADS = 4
GLA_DK = 64
GLA_DV = 128
GLA_RANK = 16
GLA_NORMALIZER = 16.0
GLA_CHUNK = 64
NSA_HEADS = 8
NSA_DH = 64
NSA_KVH = 2
NSA_G = NSA_HEADS // NSA_KVH
CMP_STRIDE = 16
CMP_LEN = 32
CMP_HIDDEN = 256
SLC_BLOCK = 64
SLC_TOPN = 16
SEL_FORCE = 1.0e4
WINDOW = 512
EPS = 1e-6
NEG = -1.0e30
PAD_SCORE = -3.0e4
MASK_BIAS = 2.0 ** 60
LANES = 128
VMEM_LIMIT = 56 * 1024 * 1024

GLA_QK = GLA_HEADS * GLA_DK
GLA_V = GLA_HEADS * GLA_DV
NSA_Q = NSA_HEADS * NSA_DH
NSA_KV = 6 * NSA_KVH * NSA_DH
NSA_GATE = 3 * NSA_HEADS
KV_LANES = NSA_KVH * NSA_DH


def _sds(shape, dtype=F32):
    return jax.ShapeDtypeStruct(shape, dtype)


def _params(sem):
    return pltpu.CompilerParams(dimension_semantics=sem, vmem_limit_bytes=VMEM_LIMIT)


def _dot(a, b):
    return jnp.dot(a.astype(BF16), b.astype(BF16), preferred_element_type=F32)


def _dot_nt(a, b):
    return lax.dot_general(a.astype(BF16), b.astype(BF16), (((1,), (1,)), ((), ())),
                           preferred_element_type=F32)


def _dot_tn(a, b):
    return lax.dot_general(a.astype(BF16), b.astype(BF16), (((0,), (0,)), ((), ())),
                           preferred_element_type=F32)


def _split3(x):
    p1 = x.astype(BF16)
    r1 = x - p1.astype(F32)
    p2 = r1.astype(BF16)
    p3 = (r1 - p2.astype(F32)).astype(BF16)
    return p1, p2, p3


def _dot_f32(a, b):
    a1, a2, _ = _split3(a)
    b1, b2, _ = _split3(b)
    return (jnp.dot(a1, b1, preferred_element_type=F32)
            + jnp.dot(a1, b2, preferred_element_type=F32)
            + jnp.dot(a2, b1, preferred_element_type=F32))


def _dot_exact_lhs(a_bf16, x):
    out = None
    for p in _split3(x):
        t = jnp.dot(a_bf16, p, preferred_element_type=F32)
        out = t if out is None else out + t
    return out


def _dot_exact_rhs(x, b_bf16):
    out = None
    for p in _split3(x):
        t = jnp.dot(p, b_bf16, preferred_element_type=F32)
        out = t if out is None else out + t
    return out


def _sigmoid(x):
    return 1.0 / (1.0 + jnp.exp(-x))


def _silu(x):
    return x * _sigmoid(x)


def _rms(x, g):
    return x * lax.rsqrt(jnp.mean(x * x, axis=-1, keepdims=True) + EPS) * g


def _ada_body(c_ref, w_ref, b_ref, o_ref):
    o_ref[...] = _dot_f32(_silu(c_ref[...]), w_ref[...]) + b_ref[...]


def _ada(c, w_ada, b_ada):
    n, d = c.shape
    nout = w_ada.shape[1]
    tn = d
    return pl.pallas_call(
        _ada_body,
        grid=(nout // tn,),
        in_specs=[pl.BlockSpec((n, d), lambda j: (0, 0)),
                  pl.BlockSpec((d, tn), lambda j: (0, j)),
                  pl.BlockSpec((1, tn), lambda j: (0, j))],
        out_specs=pl.BlockSpec((n, tn), lambda j: (0, j)),
        out_shape=_sds((n, nout)),
        compiler_params=_params(("arbitrary",)),
        name="ada",
    )(c, w_ada, b_ada.reshape(1, nout))


_C_ZQ = 0
_C_ZK = _C_ZQ + GLA_QK
_C_ZV = _C_ZK + GLA_QK
_C_ZR = _C_ZV + GLA_V
_C_ZQN = _C_ZR + GLA_V
_C_ZKV = _C_ZQN + NSA_Q
_C_MISC = _C_ZKV + NSA_KV
_C_END = _C_MISC + LANES
GATE_COL0 = GLA_RANK


def _inproj_body(x_ref, sc_ref, sh_ref, g_ref, w_ref, wg_ref, bg_ref,
                 qg_ref, kg_ref, vg_ref, rg_ref, qn_ref, zkv_ref, logf_ref, gts_ref):
    h = _rms(x_ref[0], g_ref[...]) * (1.0 + sc_ref[0]) + sh_ref[0]
    hb = h.astype(BF16)

    def proj(a, b):
        return jnp.dot(hb, w_ref[:, a:b], preferred_element_type=F32)

    qg_ref[0] = proj(_C_ZQ, _C_ZK) * (GLA_DK ** -0.5)
    kg_ref[0] = proj(_C_ZK, _C_ZV)
    vg_ref[0] = proj(_C_ZV, _C_ZR)
    rg_ref[0] = _silu(proj(_C_ZR, _C_ZQN))
    qn_ref[0] = proj(_C_ZQN, _C_ZKV) * (NSA_DH ** -0.5)
    zkv_ref[0] = proj(_C_ZKV, _C_MISC)
    zm = proj(_C_MISC, _C_END)
    gts_ref[0] = _sigmoid(zm)
    logit = jnp.dot(zm.astype(BF16), wg_ref[...], preferred_element_type=F32) + bg_ref[...]
    log_sig = jnp.minimum(logit, 0.0) - jnp.log(1.0 + jnp.exp(-jnp.abs(logit)))
    logf_ref[0] = log_sig * (1.0 / GLA_NORMALIZER)


def _mod_spec(mod, tq):
    d = mod.shape[-1]
    if mod.shape[1] == 1:
        return pl.BlockSpec((1, 1, d), lambda b, t: (b, 0, 0))
    return pl.BlockSpec((1, tq, d), lambda b, t: (b, t, 0))


def _inproj(x, sc, sh, gain, w_perm, wg_pad, bg):
    bsz, t, d = x.shape
    tq = min(t, 512)
    assert t % tq == 0
    widths = (GLA_QK, GLA_QK, GLA_V, GLA_V, NSA_Q, NSA_KV, GLA_QK, LANES)
    tok = lambda n: pl.BlockSpec((1, tq, n), lambda b, i: (b, i, 0))
    full = lambda a: pl.BlockSpec(a.shape, lambda b, i: (0,) * a.ndim)
    return pl.pallas_call(
        _inproj_body,
        grid=(bsz, t // tq),
        in_specs=[tok(d), _mod_spec(sc, tq), _mod_spec(sh, tq), full(gain), full(w_perm),
                  full(wg_pad), full(bg)],
        out_specs=[tok(n) for n in widths],
        out_shape=[_sds((bsz, t, n)) for n in widths],
        compiler_params=_params(("arbitrary", "arbitrary")),
        name="inproj",
    )(x, sc, sh, gain, w_perm, wg_pad, bg)


def _gla_body(q_ref, k_ref, v_ref, f_ref, r_ref, s0_ref, gn_ref, o_ref, sfin_ref, s_scr,
              *, rows, nchunks):
    c_len = GLA_CHUNK
    t = pl.program_id(1)

    @pl.when(t == 0)
    def _():
        s_scr[...] = s0_ref[0]

    ri = lax.broadcasted_iota(jnp.int32, (c_len, c_len), 0)
    ci = lax.broadcasted_iota(jnp.int32, (c_len, c_len), 1)
    causal = ci <= ri
    tril = jnp.where(causal, 1.0, 0.0).astype(BF16)
    ones_cv = jnp.ones((c_len, GLA_DV), BF16)
    valid = min(rows, c_len)

    def load(ref, c):
        x = ref[0, c * c_len:c * c_len + valid, :]
        if valid < c_len:
            x = jnp.concatenate([x, jnp.zeros((c_len - valid, x.shape[1]), F32)], axis=0)
        return x

    for c in range(nchunks):
        q, k, v, g, r = (load(ref, c) for ref in (q_ref, k_ref, v_ref, f_ref, r_ref))
        gparts = _split3(g)
        b = None
        for gp in gparts:
            term = jnp.dot(tril, gp, preferred_element_type=F32)
            b = term if b is None else b + term
        b_last = b[c_len - 1:c_len, :]
        b_mid = b[c_len // 2 - 1:c_len // 2, :]
        q_inter = q * jnp.exp(b)
        q_intra = q * jnp.exp(b - b_mid)
        k_intra = k * jnp.exp(b_mid - b)
        k_out = k * jnp.exp(b_last - b)
        for h in range(GLA_HEADS):
            ks = slice(h * GLA_DK, (h + 1) * GLA_DK)
            vs = slice(h * GLA_DV, (h + 1) * GLA_DV)
            a = jnp.where(causal, _dot_nt(q_intra[:, ks], k_intra[:, ks]), 0.0)
            s_old = s_scr[h]
            o = _dot(a, v[:, vs]) + _dot(q_inter[:, ks], s_old)
            tot = None
            for gp in gparts:
                term = lax.dot_general(gp[:, ks], ones_cv, (((0,), (0,)), ((), ())),
                                       preferred_element_type=F32)
                tot = term if tot is None else tot + term
            s_scr[h] = s_old * jnp.exp(tot) + _dot_tn(k_out[:, ks], v[:, vs])
            on = _rms(o, gn_ref[...]) * r[:, vs]
            o_ref[0, c * c_len:c * c_len + valid, vs] = on[:valid]

    @pl.when(t == pl.num_programs(1) - 1)
    def _():
        sfin_ref[0] = s_scr[...]


def _gla(qg, kg, vg, logf, rg, s0, gnorm):
    bsz, t, _ = qg.shape
    tt = min(t, 4 * GLA_CHUNK)
    assert t % tt == 0 and (tt % GLA_CHUNK == 0 or tt < GLA_CHUNK)
    nchunks = max(tt // GLA_CHUNK, 1)
    tok = lambda n: pl.BlockSpec((1, tt, n), lambda b, i: (b, i, 0))
    st = pl.BlockSpec((1, GLA_HEADS, GLA_DK, GLA_DV), lambda b, i: (b, 0, 0, 0))
    return pl.pallas_call(
        functools.partial(_gla_body, rows=tt, nchunks=nchunks),
        grid=(bsz, t // tt),
        in_specs=[tok(GLA_QK), tok(GLA_QK), tok(GLA_V), tok(GLA_QK), tok(GLA_V), st,
                  pl.BlockSpec((1, GLA_DV), lambda b, i: (0, 0))],
        out_specs=[tok(GLA_V), st],
        out_shape=[_sds((bsz, t, GLA_V)), _sds((bsz, GLA_HEADS, GLA_DK, GLA_DV))],
        scratch_shapes=[pltpu.VMEM((GLA_HEADS, GLA_DK, GLA_DV), F32)],
        compiler_params=_params(("arbitrary", "arbitrary")),
        name="gla",
    )(qg, kg, vg, logf, rg, s0, gnorm)


def _posbias_body(p_ref, w_ref, o_ref):
    o_ref[0, 0] = _dot_f32(p_ref[0, 0], w_ref[0, 0])


def _posbias(cmp_pos, cmp_w1):
    kdim = CMP_STRIDE * NSA_DH
    pos = jnp.broadcast_to(cmp_pos.reshape(2, 2, 1, kdim), (2, 2, 8, kdim))
    w = cmp_w1.reshape(2, 2, kdim, CMP_HIDDEN)
    return pl.pallas_call(
        _posbias_body,
        grid=(2, 2),
        in_specs=[pl.BlockSpec((1, 1, 8, kdim), lambda a, b: (a, b, 0, 0)),
                  pl.BlockSpec((1, 1, kdim, CMP_HIDDEN), lambda a, b: (a, b, 0, 0))],
        out_specs=pl.BlockSpec((1, 1, 8, CMP_HIDDEN), lambda a, b: (a, b, 0, 0)),
        out_shape=_sds((2, 2, 8, CMP_HIDDEN)),
        compiler_params=_params(("arbitrary", "arbitrary")),
        name="posbias",
    )(pos, w)


def _compress_rows(x_ref, w1_ref, pb_ref, b1_ref, w2_ref, b2_ref):
    nch = x_ref.shape[0] // CMP_STRIDE
    lo = lax.broadcasted_iota(jnp.int32, (nch, KV_LANES), 1) < NSA_DH
    cols = [[], []]
    for jp in range(CMP_STRIDE // 2):
        xe = x_ref[pl.ds(2 * jp, nch, stride=CMP_STRIDE), :]
        xo = x_ref[pl.ds(2 * jp + 1, nch, stride=CMP_STRIDE), :]
        cols[0].append(jnp.where(lo, xe, pltpu.roll(xo, NSA_DH, axis=1)).astype(BF16))
        cols[1].append(jnp.where(lo, pltpu.roll(xe, NSA_DH, axis=1), xo).astype(BF16))
    bias = pb_ref[0, 0, 0:1, :] + pb_ref[0, 1, 0:1, :] + b1_ref[0]
    outs = []
    for h in range(NSA_KVH):
        z = jnp.dot(jnp.concatenate(cols[h], axis=1), w1_ref[0], preferred_element_type=F32)
        z_second = z[:, CMP_HIDDEN:2 * CMP_HIDDEN]
        z_next = jnp.concatenate([z_second[1:], jnp.zeros((1, CMP_HIDDEN), F32)], axis=0)
        pre = z[:, 0:CMP_HIDDEN] + z_next + bias
        outs.append(_dot(_silu(pre), w2_ref[0]) + b2_ref[0])
    return jnp.concatenate(outs, axis=1)


def _compress_weights(cmp_w1, cmp_w2):
    kdim = CMP_STRIDE * NSA_DH
    w1 = jnp.concatenate([cmp_w1[:, :CMP_STRIDE].reshape(2, kdim, CMP_HIDDEN),
                          cmp_w1[:, CMP_STRIDE:].reshape(2, kdim, CMP_HIDDEN)], axis=-1)
    return w1.astype(BF16), cmp_w2.astype(BF16)


def _compress_wspec(shape):
    nd = len(shape)
    return pl.BlockSpec((1,) + shape[1:], lambda b, cc, *_: (cc,) + (0,) * (nd - 1))


def _compress_prompt_body(x_ref, w1_ref, pb_ref, b1_ref, w2_ref, b2_ref, o_ref):
    o_ref[0, 0] = _compress_rows(x_ref.at[0], w1_ref, pb_ref, b1_ref, w2_ref, b2_ref)


def _compress_prompt(zkv, w1, posb, b1, w2, b2):
    bsz, t, _ = zkv.shape
    assert t % LANES == 0
    nch = t // CMP_STRIDE
    weights = (w1, posb, b1, w2, b2)
    return pl.pallas_call(
        _compress_prompt_body,
        grid=(bsz, 2),
        in_specs=[pl.BlockSpec((1, t, KV_LANES), lambda b, cc: (b, 0, cc))]
        + [_compress_wspec(a.shape) for a in weights],
        out_specs=pl.BlockSpec((1, 1, nch, KV_LANES), lambda b, cc: (cc, b, 0, 0)),
        out_shape=_sds((2, bsz, nch, KV_LANES)),
        compiler_params=_params(("arbitrary", "arbitrary")),
        name="compress_prompt",
    )(zkv, *weights)


def _compress_paged_body(pt_ref, cache_ref, w1_ref, pb_ref, b1_ref, w2_ref, b2_ref, o_ref,
                         xt_buf, x_scr, sem, *, npages):
    b = pl.program_id(0)
    cc = pl.program_id(1)
    step = b * 2 + cc
    nsteps = pl.num_programs(0) * 2
    page = xt_buf.shape[-1]

    def page_copy(bb, c, slot, p):
        return pltpu.make_async_copy(cache_ref.at[pt_ref[bb, p], c], xt_buf.at[slot, p],
                                     sem.at[slot])

    @pl.when(step == 0)
    def _():
        for p in range(npages):
            page_copy(0, 0, 0, p).start()

    @pl.when(step + 1 < nsteps)
    def _():
        nxt = step + 1
        for p in range(npages):
            page_copy(nxt // 2, nxt % 2, nxt % 2, p).start()

    slot = step % 2
    for p in range(npages):
        page_copy(b, cc, slot, p).wait()
    for p in range(npages):
        x_scr[p * page:(p + 1) * page, :] = xt_buf[slot, p].T
    o_ref[0, 0] = _compress_rows(x_scr, w1_ref, pb_ref, b1_ref, w2_ref, b2_ref)


def _compress_paged(cache_t, page_table, w1, posb, b1, w2, b2):
    bsz, npages = page_table.shape
    page = cache_t.shape[-1]
    assert page == LANES and cache_t.shape[2] == KV_LANES
    nch = npages * page // CMP_STRIDE
    weights = (w1, posb, b1, w2, b2)
    grid_spec = pltpu.PrefetchScalarGridSpec(
        num_scalar_prefetch=1,
        grid=(bsz, 2),
        in_specs=[pl.BlockSpec(memory_space=pl.ANY)] + [_compress_wspec(a.shape) for a in weights],
        out_specs=pl.BlockSpec((1, 1, nch, KV_LANES), lambda b, cc, pt: (cc, b, 0, 0)),
        scratch_shapes=[pltpu.VMEM((2, npages, KV_LANES, page), F32),
                        pltpu.VMEM((npages * page, KV_LANES), F32),
                        pltpu.SemaphoreType.DMA((2,))],
    )
    return pl.pallas_call(
        functools.partial(_compress_paged_body, npages=npages),
        grid_spec=grid_spec,
        out_shape=_sds((2, bsz, nch, KV_LANES)),
        compiler_params=_params(("arbitrary", "arbitrary")),
        name="compress_paged",
    )(page_table, cache_t, *weights)


def _cmp_to_slc_map(nc, ns):
    start = np.arange(nc) * CMP_STRIDE
    bs = np.arange(ns) * SLC_BLOCK
    ov = (np.minimum(start[:, None] + CMP_LEN, bs[None, :] + SLC_BLOCK)
          - np.maximum(start[:, None], bs[None, :]))
    return (np.clip(ov, 0, None) / CMP_LEN).astype(np.float32)


def _select_top(score_t, ns, n_sel):
    nrow, nq = score_t.shape
    groups = [score_t[8 * v:8 * v + 8, :] for v in range(nrow // 8)]
    ranks = [jnp.zeros((8, nq), F32) for _ in groups]
    sub = lax.broadcasted_iota(jnp.int32, (8, nq), 0)
    for i in range(ns):
        row = jnp.broadcast_to(score_t[i:i + 1, :], (8, nq))
        for v, grp in enumerate(groups):
            if i < 8 * v:
                ranks[v] = jnp.where(row >= grp, ranks[v] + 1.0, ranks[v])
            elif i >= 8 * v + 8:
                ranks[v] = jnp.where(row > grp, ranks[v] + 1.0, ranks[v])
            else:
                later = jnp.where(row >= grp, 1.0, 0.0)
                earlier = jnp.where(row > grp, 1.0, 0.0)
                ranks[v] = ranks[v] + jnp.where(sub > (i - 8 * v), later, earlier)
    rank = jnp.concatenate(ranks, axis=0)
    return jnp.where(rank < float(n_sel), 1.0, 0.0)


def _nsa_prompt_body(q_ref, kc_ref, vc_ref, ks_ref, vs_ref, kw_ref, vw_ref, g_ref,
                     slope_ref, dq_ref, slope_c_ref, dc_ref, mmap_ref, kfeat_ref,
                     o_ref, qs_scr, qw_scr, m_scr, acc_scr, oc_scr, *, tq, ns, n_sel):
    i = pl.program_id(1)
    t0 = i * tq
    tk = tq
    m_rows = NSA_G * tq
    nwin = WINDOW // tk
    lane_q = lax.broadcasted_iota(jnp.int32, (tq, KV_LANES), 1)
    lane_m = lax.broadcasted_iota(jnp.int32, (m_rows, LANES), 1)
    lane_v = lax.broadcasted_iota(jnp.int32, (tk, KV_LANES), 1)
    dq = dq_ref[...]
    rel_blk = ((lane_m - t0 // SLC_BLOCK) * SLC_BLOCK).astype(F32)

    for h in range(NSA_KVH):
        qs = []
        for g in range(NSA_G):
            c0 = (h * NSA_G + g) * NSA_DH
            qg = q_ref[0, :, c0:c0 + NSA_DH]
            qq = jnp.concatenate([qg, qg], axis=1)
            qs.append(jnp.where((lane_q // NSA_DH) == h, qq, 0.0))
        qb = jnp.concatenate(qs, axis=0).astype(BF16)

        d_c = dc_ref[...] + t0.astype(F32)
        s_c = _dot_nt(qb, kc_ref[0, 0])
        s_c = jnp.where(d_c >= 0.0, s_c - slope_c_ref[h] * d_c, NEG)
        mx = jnp.max(s_c, axis=-1, keepdims=True)
        p = jnp.where(s_c > 0.5 * NEG, jnp.exp(s_c - mx), 0.0)
        p_c = p * (1.0 / jnp.maximum(jnp.sum(p, axis=-1, keepdims=True), 1e-30))
        oc_scr[h] = _dot(p_c, vc_ref[0, 0])

        p_sum = p_c[0:tq]
        for g in range(1, NSA_G):
            p_sum = p_sum + p_c[g * tq:(g + 1) * tq]
        imp = _dot_exact_rhs(p_sum, mmap_ref[...])
        blk = lax.broadcasted_iota(jnp.int32, (tq, ns), 1)
        cur = (t0 + lax.broadcasted_iota(jnp.int32, (tq, ns), 0)) // SLC_BLOCK
        forced = (blk == 0) | (blk == cur) | (blk == cur - 1)
        score = jnp.where(blk <= cur, jnp.where(forced, SEL_FORCE, imp), -SEL_FORCE)
        score = jnp.concatenate([score, jnp.full((tq, LANES - ns), PAD_SCORE, F32)], axis=1)
        ns8 = -(-ns // 8) * 8
        sel_t = _select_top(score.T[0:ns8], ns, n_sel)
        sel = jnp.concatenate([sel_t, jnp.zeros((LANES - ns8, tq), F32)], axis=0).T
        sel = jnp.concatenate([sel] * NSA_G, axis=0)

        slope = slope_ref[h]
        feat_w = jnp.where(lane_m < ns, slope * rel_blk, jnp.where(lane_m == ns, slope, 0.0))
        feat_s = jnp.where(lane_m < ns, jnp.where(sel > 0.5, feat_w, -MASK_BIAS), feat_w)
        qs_scr[h] = jnp.concatenate([qb, feat_s.astype(BF16)], axis=1)
        qw_scr[h] = jnp.concatenate([qb, feat_w.astype(BF16)], axis=1)

    def tile(h, j, k_ref, v_ref, q_scr, mask):
        k0 = pl.multiple_of(j * tk, tk)
        kt = jnp.concatenate([k_ref[0, pl.ds(k0, tk), :].astype(BF16),
                              kfeat_ref[pl.ds(k0, tk), :]], axis=1)
        s = lax.dot_general(q_scr[h], kt, (((1,), (1,)), ((), ())), preferred_element_type=F32)
        if mask == "causal":
            s = jnp.where(dq >= 0.0, s, NEG)
        elif mask == "far":
            s = jnp.where(dq <= 0.0, s, NEG)
        vt = jnp.where((lane_v // NSA_DH) == h, v_ref[0, pl.ds(k0, tk), :], 1.0).astype(BF16)
        m_old = m_scr[h]
        m_new = jnp.maximum(m_old, jnp.max(s, axis=-1, keepdims=True))
        p = jnp.exp(s - jnp.concatenate([m_new] * (tk // LANES), axis=1))
        acc_scr[h] = (jnp.exp(m_old - m_new) * acc_scr[h]
                      + jnp.dot(p.astype(BF16), vt, preferred_element_type=F32))
        m_scr[h] = m_new

    def reset():
        m_scr[...] = jnp.full(m_scr.shape, NEG, F32)
        acc_scr[...] = jnp.zeros(acc_scr.shape, F32)

    def finish(h):
        acc = acc_scr[h]
        return acc * (1.0 / pltpu.roll(acc, NSA_DH, axis=1))

    reset()
    for h in range(NSA_KVH):
        tile(h, i, ks_ref, vs_ref, qs_scr, "causal")

    def sel_body(jj, carry):
        for h in range(NSA_KVH):
            tile(h, i - 1 - jj, ks_ref, vs_ref, qs_scr, None)
        return carry

    lax.fori_loop(0, i, sel_body, 0)
    o_s = [finish(h) for h in range(NSA_KVH)]

    reset()
    for h in range(NSA_KVH):
        tile(h, i, kw_ref, vw_ref, qw_scr, "causal")
    for d in range(1, nwin + 1):
        @pl.when(i >= d)
        def _():
            for h in range(NSA_KVH):
                tile(h, i - d, kw_ref, vw_ref, qw_scr, "far" if d == nwin else None)
    o_w = [finish(h) for h in range(NSA_KVH)]

    gates = g_ref[0]
    out_cols = []
    for h in range(NSA_KVH):
        o_c = oc_scr[h]
        for g in range(NSA_G):
            col = GATE_COL0 + (h * NSA_G + g) * 3
            rs = slice(g * tq, (g + 1) * tq)
            hs = slice(h * NSA_DH, (h + 1) * NSA_DH)
            out_cols.append(gates[:, col:col + 1] * o_c[rs, hs]
                            + gates[:, col + 1:col + 2] * o_s[h][rs, hs]
                            + gates[:, col + 2:col + 3] * o_w[h][rs, hs])
    o_ref[0] = jnp.concatenate(out_cols, axis=1)


def _alibi_slopes():
    h = np.arange(1, NSA_HEADS + 1, dtype=np.float32)
    return np.exp2(-8.0 * h / NSA_HEADS).astype(np.float32).reshape(NSA_KVH, NSA_G)


def _nsa_prompt(qn, kcvc, zkv, gts):
    bsz, t, _ = qn.shape
    tq = min(t, 256)
    assert t % tq == 0 and WINDOW % tq == 0 and tq % SLC_BLOCK == 0 and tq % LANES == 0
    nch = kcvc.shape[2]
    ns = -(-t // SLC_BLOCK)
    assert ns < LANES
    n_sel = min(SLC_TOPN, ns)
    m_rows = NSA_G * tq
    slopes = _alibi_slopes()
    qi = np.tile(np.arange(tq, dtype=np.float32), NSA_G)
    slope_rows = np.repeat(slopes, tq, axis=1)
    slope_mat = np.ascontiguousarray(np.broadcast_to(slope_rows[:, :, None], (NSA_KVH, m_rows, LANES)))
    dq_mat = qi[:, None] - np.arange(tq, dtype=np.float32)[None, :]
    slope_c = np.ascontiguousarray(np.broadcast_to(slope_rows[:, :, None], (NSA_KVH, m_rows, nch)))
    cmp_end = np.arange(nch, dtype=np.float32) * CMP_STRIDE + (CMP_LEN - 1)
    dc_mat = qi[:, None] - cmp_end[None, :]
    mmap = jnp.asarray(_cmp_to_slc_map(nch, ns), BF16)
    key = np.arange(t)
    kfeat = np.zeros((t, LANES), np.float32)
    kfeat[key, key // SLC_BLOCK] = 1.0
    kfeat[:, ns] = key % SLC_BLOCK

    kv = lambda c: pl.BlockSpec((1, t, KV_LANES), lambda b, i: (b, 0, c))
    cst = lambda a: pl.BlockSpec(a.shape, lambda b, i: (0,) * a.ndim)
    consts = [jnp.asarray(slope_mat), jnp.asarray(dq_mat), jnp.asarray(slope_c),
              jnp.asarray(dc_mat), mmap, jnp.asarray(kfeat, BF16)]
    return pl.pallas_call(
        functools.partial(_nsa_prompt_body, tq=tq, ns=ns, n_sel=n_sel),
        grid=(bsz, t // tq),
        in_specs=[pl.BlockSpec((1, tq, NSA_Q), lambda b, i: (b, i, 0)),
                  pl.BlockSpec((1, 1, nch, KV_LANES), lambda b, i: (0, b, 0, 0)),
                  pl.BlockSpec((1, 1, nch, KV_LANES), lambda b, i: (1, b, 0, 0)),
                  kv(2), kv(3), kv(4), kv(5),
                  pl.BlockSpec((1, tq, LANES), lambda b, i: (b, i, 0))] + [cst(a) for a in consts],
        out_specs=pl.BlockSpec((1, tq, NSA_Q), lambda b, i: (b, i, 0)),
        out_shape=_sds((bsz, t, NSA_Q)),
        scratch_shapes=[pltpu.VMEM((NSA_KVH, m_rows, 2 * LANES), BF16),
                        pltpu.VMEM((NSA_KVH, m_rows, 2 * LANES), BF16),
                        pltpu.VMEM((NSA_KVH, m_rows, LANES), F32),
                        pltpu.VMEM((NSA_KVH, m_rows, KV_LANES), F32),
                        pltpu.VMEM((NSA_KVH, m_rows, KV_LANES), F32)],
        compiler_params=_params(("arbitrary", "arbitrary")),
        name="nsa_prompt",
    )(qn, kcvc, kcvc, zkv, zkv, zkv, zkv, gts, *consts)


def _nsa_sample_body(pt_ref, q_ref, kc_ref, vc_ref, rows_ref, swt_ref, g_ref,
                     slope_ref, pos_ref, cur_ref, dc_ref, gsum_ref, mmap_ref, expand_ref,
                     cache_ref, o_ref, wout_ref, kvbuf, sem, *, past, t, npages, ns, n_sel):
    b = pl.program_id(0)
    nb = pl.num_programs(0)
    page = past // npages
    m_rows = NSA_HEADS * t
    half = m_rows // NSA_KVH
    pw = swt_ref.shape[-1]

    def page_copy(bb, slot, p):
        return pltpu.make_async_copy(cache_ref.at[pt_ref[bb, p], pl.ds(2, 2)],
                                     kvbuf.at[slot, :, :, pl.ds(p * page, page)], sem.at[slot])

    @pl.when(b == 0)
    def _():
        for p in range(npages):
            page_copy(0, 0, p).start()

    @pl.when(b + 1 < nb)
    def _():
        for p in range(npages):
            page_copy(b + 1, (b + 1) % 2, p).start()

    pieces = []
    for hh in range(NSA_HEADS):
        qg = q_ref[0, :, hh * NSA_DH:(hh + 1) * NSA_DH]
        z = jnp.zeros_like(qg)
        pieces.append(jnp.concatenate([qg, z] if hh < NSA_G else [z, qg], axis=1))
    qbd = jnp.concatenate(pieces, axis=0).astype(BF16)
    slope = slope_ref[...]
    pos = pos_ref[...]

    def widen(x, n):
        return jnp.concatenate([x] * (n // LANES), axis=1)

    def softmax(parts):
        mx = None
        for s in parts:
            r = jnp.max(s, axis=-1, keepdims=True)
            mx = r if mx is None else jnp.maximum(mx, r)
        ps = [jnp.where(s > 0.5 * NEG, jnp.exp(s - mx), 0.0) for s in parts]
        tot = None
        for p in ps:
            r = jnp.sum(p, axis=-1, keepdims=True)
            tot = r if tot is None else tot + r
        inv = 1.0 / jnp.maximum(tot, 1e-30)
        return [p * inv for p in ps]

    d_c = dc_ref[...]
    nch = d_c.shape[1]
    s_c = _dot_nt(qbd, kc_ref[0, 0])
    (p_c,) = softmax([jnp.where(d_c >= 0.0, s_c - widen(slope, nch) * d_c, NEG)])
    o_c = _dot(p_c, vc_ref[0, 0])

    imp = _dot_exact_lhs(gsum_ref[...], _dot_exact_rhs(p_c, mmap_ref[...]))
    ns8 = imp.shape[1]
    imp = jnp.concatenate([imp, jnp.zeros((m_rows, 2 * LANES - ns8), F32)], axis=1)
    imp = jnp.concatenate([imp, jnp.zeros((LANES - m_rows, 2 * LANES), F32)], axis=0)
    imp_t = imp.T[0:ns8, 0:m_rows]
    blk = lax.broadcasted_iota(jnp.int32, (ns8, m_rows), 0)
    cur = cur_ref[...]
    forced = (blk == 0) | (blk == cur) | (blk == cur - 1)
    score = jnp.where(blk <= cur, jnp.where(forced, SEL_FORCE, imp_t), -SEL_FORCE)
    score = jnp.where(blk < ns, score, PAD_SCORE)
    sel_t = _select_top(score, ns, n_sel)
    sel_t = jnp.concatenate([sel_t, jnp.zeros((ns8, LANES - m_rows), F32)], axis=1)
    sel_t = jnp.concatenate([sel_t, jnp.zeros((2 * LANES - ns8, LANES), F32)], axis=0)
    sel = sel_t.T[0:m_rows, 0:ns8]

    new_ks = rows_ref[0, :, 2 * KV_LANES:3 * KV_LANES]
    new_vs = rows_ref[0, :, 3 * KV_LANES:4 * KV_LANES]
    new_kw = rows_ref[0, :, 4 * KV_LANES:5 * KV_LANES]
    new_vw = rows_ref[0, :, 5 * KV_LANES:6 * KV_LANES]
    dist_n = pos[:, 0:t] - (past + lax.broadcasted_iota(jnp.int32, (m_rows, t), 1)).astype(F32)
    slope_n = slope[:, 0:t]

    kwt = swt_ref[0, 0]
    vwt = swt_ref[0, 1]
    dist_w = widen(pos, pw) - (lax.broadcasted_iota(jnp.int32, (m_rows, pw), 1)
                               + (past - pw)).astype(F32)
    s_w = _dot(qbd, kwt)
    s_w = jnp.where(dist_w >= 0.0,
                    jnp.where(dist_w <= float(WINDOW), s_w - widen(slope, pw) * dist_w, NEG), NEG)
    s_wn = _dot_nt(qbd, new_kw)
    s_wn = jnp.where(dist_n >= 0.0,
                     jnp.where(dist_n <= float(WINDOW), s_wn - slope_n * dist_n, NEG), NEG)
    p_w, p_wn = softmax([s_w, s_wn])
    o_w = _dot_nt(p_w, vwt) + _dot(p_wn, new_vw)
    w_rows = jnp.concatenate([jnp.concatenate([kwt.T, vwt.T], axis=1),
                              rows_ref[0, :, 4 * KV_LANES:6 * KV_LANES]], axis=0)
    wout_ref[0] = w_rows[pw + t - wout_ref.shape[1]:, :]

    slot = b % 2
    for p in range(npages):
        page_copy(b, slot, p).wait()
    nfull = past // SLC_BLOCK
    mask = jnp.dot(sel.astype(BF16), expand_ref[...], preferred_element_type=F32)
    dist = widen(pos, past) - lax.broadcasted_iota(jnp.int32, (m_rows, past), 1).astype(F32)
    s_s = _dot(qbd, kvbuf[slot, 0])
    s_s = jnp.where(mask > 0.5, jnp.where(dist >= 0.0, s_s - widen(slope, past) * dist, NEG), NEG)
    s_sn = _dot_nt(qbd, new_ks)
    s_sn = jnp.where(sel[:, nfull:nfull + 1] > 0.5,
                     jnp.where(dist_n >= 0.0, s_sn - slope_n * dist_n, NEG), NEG)
    p_s, p_sn = softmax([s_s, s_sn])
    o_s = _dot_nt(p_s, kvbuf[slot, 1]) + _dot(p_sn, new_vs)

    def own(x):
        return jnp.concatenate([x[0:half, 0:NSA_DH], x[half:, NSA_DH:2 * NSA_DH]], axis=0)

    gates = g_ref[0]

    def gate_col(br):
        return jnp.concatenate(
            [gates[:, GATE_COL0 + hh * 3 + br:GATE_COL0 + hh * 3 + br + 1] for hh in range(NSA_HEADS)],
            axis=0)

    mix = gate_col(0) * own(o_c) + gate_col(1) * own(o_s) + gate_col(2) * own(o_w)
    o_ref[0] = jnp.concatenate([mix[hh * t:(hh + 1) * t, :] for hh in range(NSA_HEADS)], axis=1)


def _nsa_sample(qn, kcvc, rows, state_win_t, gts, cache_t, page_table):
    bsz, t, _ = qn.shape
    npages = page_table.shape[1]
    page = cache_t.shape[-1]
    past = npages * page
    pw = state_win_t.shape[-1]
    nch = kcvc.shape[2]
    ns = -(-(past + t) // SLC_BLOCK)
    ns8 = -(-ns // 8) * 8
    m_rows = NSA_HEADS * t
    assert past % SLC_BLOCK == 0 and t <= SLC_BLOCK and t % 8 == 0 and pw + t >= WINDOW
    assert m_rows <= LANES and ns8 <= 2 * LANES and nch % LANES == 0 and pw % LANES == 0
    n_sel = min(SLC_TOPN, ns)
    slopes = _alibi_slopes().reshape(-1)
    lanes1 = np.ones((1, LANES), np.float32)
    slope_b = np.repeat(slopes, t)[:, None].astype(np.float32) * lanes1
    pos_i = np.tile(past + np.arange(t), NSA_HEADS)
    cmp_end = np.arange(nch, dtype=np.float32) * CMP_STRIDE + (CMP_LEN - 1)
    dc_mat = pos_i[:, None].astype(np.float32) - cmp_end[None, :]
    row_kvh = np.arange(m_rows) // (NSA_G * t)
    row_tok = np.arange(m_rows) % t
    gsum = ((row_kvh[:, None] == row_kvh[None, :]) & (row_tok[:, None] == row_tok[None, :]))
    mmap = np.zeros((nch, ns8), np.float32)
    mmap[:, :ns] = _cmp_to_slc_map(nch, ns)
    expand = (np.arange(ns8)[:, None] == (np.arange(past) // SLC_BLOCK)[None, :])
    consts = [jnp.asarray(slope_b), jnp.asarray(pos_i[:, None].astype(np.float32) * lanes1),
              jnp.asarray((pos_i // SLC_BLOCK)[None, :].astype(np.int32)), jnp.asarray(dc_mat),
              jnp.asarray(gsum.astype(np.float32), BF16), jnp.asarray(mmap, BF16),
              jnp.asarray(expand.astype(np.float32), BF16)]

    cst = lambda a: pl.BlockSpec(a.shape, lambda b, pt: (0,) * a.ndim)
    grid_spec = pltpu.PrefetchScalarGridSpec(
        num_scalar_prefetch=1,
        grid=(bsz,),
        in_specs=[pl.BlockSpec((1, t, NSA_Q), lambda b, pt: (b, 0, 0)),
                  pl.BlockSpec((1, 1, nch, KV_LANES), lambda b, pt: (0, b, 0, 0)),
                  pl.BlockSpec((1, 1, nch, KV_LANES), lambda b, pt: (1, b, 0, 0)),
                  pl.BlockSpec((1, t, NSA_KV), lambda b, pt: (b, 0, 0)),
                  pl.BlockSpec((1, 2, KV_LANES, pw), lambda b, pt: (b, 0, 0, 0)),
                  pl.BlockSpec((1, t, LANES), lambda b, pt: (b, 0, 0))]
        + [cst(a) for a in consts] + [pl.BlockSpec(memory_space=pl.ANY)],
        out_specs=[pl.BlockSpec((1, t, NSA_Q), lambda b, pt: (b, 0, 0)),
                   pl.BlockSpec((1, WINDOW, 2 * KV_LANES), lambda b, pt: (b, 0, 0))],
        scratch_shapes=[pltpu.VMEM((2, 2, KV_LANES, past), F32), pltpu.SemaphoreType.DMA((2,))],
    )
    return pl.pallas_call(
        functools.partial(_nsa_sample_body, past=past, t=t, npages=npages, ns=ns, n_sel=n_sel),
        grid_spec=grid_spec,
        out_shape=[_sds((bsz, t, NSA_Q)), _sds((bsz, WINDOW, 2 * KV_LANES))],
        compiler_params=_params(("arbitrary",)),
        name="nsa_sample",
    )(page_table, qn, kcvc, kcvc, rows, state_win_t, gts, *consts, cache_t)


def _ffn_body(x_ref, og_ref, on_ref, gtm_ref, scf_ref, shf_ref, gtf_ref,
              nmp_ref, nfp_ref, nfo_ref, wo_ref, wu_ref, wd_ref, y_ref, *, ff_chunk):
    mix = (jnp.dot(og_ref[0].astype(BF16), wo_ref[0:GLA_V, :], preferred_element_type=F32)
           + jnp.dot(on_ref[0].astype(BF16), wo_ref[GLA_V:GLA_V + NSA_Q, :],
                     preferred_element_type=F32))
    x1 = x_ref[0] + gtm_ref[0] * _rms(mix, nmp_ref[...])
    hb = (_rms(x1, nfp_ref[...]) * (1.0 + scf_ref[0]) + shf_ref[0]).astype(BF16)
    f = None
    for c in range(wu_ref.shape[1] // ff_chunk):
        cs = slice(c * ff_chunk, (c + 1) * ff_chunk)
        u = jnp.maximum(jnp.dot(hb, wu_ref[:, cs], preferred_element_type=F32), 0.0)
        term = jnp.dot((u * u).astype(BF16), wd_ref[cs, :], preferred_element_type=F32)
        f = term if f is None else f + term
    y_ref[0] = x1 + gtf_ref[0] * _rms(f, nfo_ref[...])


def _ffn(x, og, on, gtm, scf, shf, gtf, nmp, nfp, nfo, wo, wu, wd):
    bsz, t, d = x.shape
    tq = min(t, 512)
    assert t % tq == 0
    tok = lambda n: pl.BlockSpec((1, tq, n), lambda b, i: (b, i, 0))
    full = lambda a: pl.BlockSpec(a.shape, lambda b, i: (0,) * a.ndim,
                                  pipeline_mode=pl.Buffered(1))
    vec = lambda a: pl.BlockSpec(a.shape, lambda b, i: (0,) * a.ndim)
    return pl.pallas_call(
        functools.partial(_ffn_body, ff_chunk=min(wu.shape[1], 1024)),
        grid=(bsz, t // tq),
        in_specs=[tok(d), tok(GLA_V), tok(NSA_Q), _mod_spec(gtm, tq), _mod_spec(scf, tq),
                  _mod_spec(shf, tq), _mod_spec(gtf, tq), vec(nmp), vec(nfp), vec(nfo),
                  full(wo), full(wu), full(wd)],
        out_specs=tok(d),
        out_shape=_sds((bsz, t, d)),
        compiler_params=_params(("arbitrary", "arbitrary")),
        name="ffn",
    )(x, og, on, gtm, scf, shf, gtf, nmp, nfp, nfo, wo, wu, wd)


def _permute_w_in(w_in):
    o_zq = 0
    o_zk = o_zq + GLA_QK
    o_zv = o_zk + GLA_QK
    o_za = o_zv + GLA_V
    o_zr = o_za + GLA_RANK
    o_zqn = o_zr + GLA_V
    o_zkv = o_zqn + NSA_Q
    o_zg = o_zkv + NSA_KV
    o_end = o_zg + NSA_GATE
    d = w_in.shape[0]
    pad = jnp.zeros((d, LANES - GLA_RANK - NSA_GATE), w_in.dtype)
    cols = [w_in[:, o_zq:o_za], w_in[:, o_zr:o_zg], w_in[:, o_za:o_zr], w_in[:, o_zg:o_end], pad]
    return jnp.concatenate(cols, axis=1).astype(BF16)


def _layer_weights(l, norm_mix_pre, norm_mix_post, norm_ffn_pre, norm_ffn_post, w_in,
                   gla_w_gate, gla_b_gate, gla_norm, cmp_pos, cmp_w1, cmp_b1, cmp_w2, cmp_b2,
                   w_out, w_up, w_down):
    wg_pad = jnp.zeros((LANES, GLA_QK), F32).at[:GLA_RANK].set(gla_w_gate[l]).astype(BF16)
    w1, w2 = _compress_weights(cmp_w1[l], cmp_w2[l])
    return dict(
        nmpre=norm_mix_pre[l][None, :], nmpost=norm_mix_post[l][None, :],
        nfpre=norm_ffn_pre[l][None, :], nfpost=norm_ffn_post[l][None, :],
        w_perm=_permute_w_in(w_in[l]), wg_pad=wg_pad, bg=gla_b_gate[l][None, :],
        gnorm=gla_norm[l][None, :], w1=w1, w2=w2, posb=_posbias(cmp_pos[l], cmp_w1[l]),
        b1=cmp_b1[l][:, None, :], b2=cmp_b2[l][:, None, :],
        wo=w_out[l].astype(BF16), wu=w_up[l].astype(BF16), wd=w_down[l].astype(BF16))


def _split_ada(ada, rows_per_batch):
    parts = jnp.split(ada, 6, axis=-1)
    if rows_per_batch is None:
        return [p[:, None, :] for p in parts]
    return [jnp.repeat(p, rows_per_batch, axis=0)[None] for p in parts]


def _prompt_layer(x, ada, w, wbuf):
    bsz, t, d = x.shape
    sh_m, sc_m, gt_m, sh_f, sc_f, gt_f = _split_ada(ada, None)
    qg, kg, vg, rg, qn, zkv, logf, gts = _inproj(x, sc_m, sh_m, w["nmpre"], w["w_perm"],
                                                 w["wg_pad"], w["bg"])
    s0 = jnp.zeros((bsz, GLA_HEADS, GLA_DK, GLA_DV), F32)
    og, s_fin = _gla(qg, kg, vg, logf, rg, s0, w["gnorm"])
    kcvc = _compress_prompt(zkv, w["w1"], w["posb"], w["b1"], w["w2"], w["b2"])
    on = _nsa_prompt(qn, kcvc, zkv, gts)
    y = _ffn(x, og, on, gt_m, sc_f, sh_f, gt_f, w["nmpost"], w["nfpre"], w["nfpost"],
             w["wo"], w["wu"], w["wd"])
    rows = zkv[:, :, :4 * KV_LANES].reshape(bsz, t, 4, NSA_KVH, NSA_DH)
    win = zkv[:, :, 4 * KV_LANES:].reshape(bsz, t, 2, NSA_KVH, NSA_DH)
    if t < wbuf:
        win = jnp.pad(win, ((0, 0), (wbuf - t, 0), (0, 0), (0, 0), (0, 0)))
    return y, rows, win[:, -wbuf:], s_fin


def _sample_layer(x, ada, cache, page_table, state_win, state_gla, w):
    bsz, t, d = x.shape
    n = bsz * t
    sh_m, sc_m, gt_m, sh_f, sc_f, gt_f = _split_ada(ada, t)
    outs = _inproj(x.reshape(1, n, d), sc_m, sh_m, w["nmpre"], w["w_perm"], w["wg_pad"], w["bg"])
    qg, kg, vg, rg, qn, zkv, logf, gts = [o.reshape(bsz, t, o.shape[-1]) for o in outs]
    og, s_fin = _gla(qg, kg, vg, logf, rg, state_gla, w["gnorm"])
    n_pool, page = cache.shape[0], cache.shape[1]
    cache_t = jnp.transpose(cache, (0, 2, 3, 4, 1)).reshape(n_pool, 4, KV_LANES, page)
    pw = state_win.shape[1]
    state_win_t = jnp.transpose(state_win, (0, 2, 3, 4, 1)).reshape(bsz, 2, KV_LANES, pw)
    kcvc = _compress_paged(cache_t, page_table, w["w1"], w["posb"], w["b1"], w["w2"], w["b2"])
    on, win = _nsa_sample(qn, kcvc, zkv, state_win_t, gts, cache_t, page_table)
    y = _ffn(x.reshape(1, n, d), og.reshape(1, n, GLA_V), on.reshape(1, n, NSA_Q),
             gt_m, sc_f, sh_f, gt_f, w["nmpost"], w["nfpre"], w["nfpost"],
             w["wo"], w["wu"], w["wd"])
    rows = zkv[:, :, :4 * KV_LANES].reshape(bsz, t, 4, NSA_KVH, NSA_DH)
    return (y.reshape(bsz, t, d), rows, win.reshape(bsz, WINDOW, 2, NSA_KVH, NSA_DH), s_fin)


def kernel(x_prompt, x_sample, cache_kv, state_win, state_gla, page_table, c_prompt, c_sample,
           norm_mix_pre, norm_mix_post, norm_ffn_pre, norm_ffn_post, w_ada, b_ada, w_in,
           gla_w_gate, gla_b_gate, gla_norm, cmp_pos, cmp_w1, cmp_b1, cmp_w2, cmp_b2,
           w_out, w_up, w_down):
    depth = w_in.shape[0]
    bsz = x_prompt.shape[0]
    wbuf = state_win.shape[2]
    assert wbuf == WINDOW
    c_all = jnp.concatenate([c_prompt, c_sample], axis=0)
    y_p, y_s = x_prompt, x_sample
    outs = [[] for _ in range(6)]
    for l in range(depth):
        w = _layer_weights(l, norm_mix_pre, norm_mix_post, norm_ffn_pre, norm_ffn_post, w_in,
                           gla_w_gate, gla_b_gate, gla_norm, cmp_pos, cmp_w1, cmp_b1, cmp_w2,
                           cmp_b2, w_out, w_up, w_down)
        ada = _ada(c_all, w_ada[l], b_ada[l])
        y_p, r_p, w_p, s_p = _prompt_layer(y_p, ada[:bsz], w, wbuf)
        y_s, r_s, w_s, s_s = _sample_layer(y_s, ada[bsz:], cache_kv[l], page_table,
                                           state_win[l], state_gla[l], w)
        for lst, val in zip(outs, (r_p, r_s, w_p, w_s, s_p, s_s)):
            lst.append(val)
    return (y_p, y_s) + tuple(jnp.stack(o) for o in outs)
```

```python
import functools

import numpy as np
import jax
import jax.numpy as jnp
from jax import lax
from jax.experimental import pallas as pl
from jax.experimental.pallas import tpu as pltpu

F32 = jnp.float32
BF16 = jnp.bfloat16

GLA_HEADS = 4
GLA_DK = 64
GLA_DV = 128
GLA_RANK = 16
GLA_NORMALIZER = 16.0
GLA_CHUNK = 64
NSA_HEADS = 8
NSA_DH = 64
NSA_KVH = 2
NSA_G = NSA_HEADS // NSA_KVH
CMP_STRIDE = 16
CMP_LEN = 32
CMP_HIDDEN = 256
SLC_BLOCK = 64
SLC_TOPN = 16
SEL_FORCE = 1.0e4
WINDOW = 512
EPS = 1e-6
NEG = -1.0e30
PAD_SCORE = -3.0e4
MASK_BIAS = 2.0 ** 60
LANES = 128
VMEM_LIMIT = 56 * 1024 * 1024

GLA_QK = GLA_HEADS * GLA_DK
GLA_V = GLA_HEADS * GLA_DV
NSA_Q = NSA_HEADS * NSA_DH
NSA_KV = 6 * NSA_KVH * NSA_DH
NSA_GATE = 3 * NSA_HEADS
KV_LANES = NSA_KVH * NSA_DH
ROWS_LANES = 4 * KV_LANES


def _sds(shape, dtype=F32):
    return jax.ShapeDtypeStruct(shape, dtype)


def _params(sem):
    return pltpu.CompilerParams(dimension_semantics=sem, vmem_limit_bytes=VMEM_LIMIT)


def _dot(a, b):
    return jnp.dot(a.astype(BF16), b.astype(BF16), preferred_element_type=F32)


def _dot_nt(a, b):
    return lax.dot_general(a.astype(BF16), b.astype(BF16), (((1,), (1,)), ((), ())),
                           preferred_element_type=F32)


def _dot_tn(a, b):
    return lax.dot_general(a.astype(BF16), b.astype(BF16), (((0,), (0,)), ((), ())),
                           preferred_element_type=F32)


def _split3(x):
    p1 = x.astype(BF16)
    r1 = x - p1.astype(F32)
    p2 = r1.astype(BF16)
    p3 = (r1 - p2.astype(F32)).astype(BF16)
    return p1, p2, p3


def _dot_f32(a, b):
    a1, a2, _ = _split3(a)
    b1, b2, _ = _split3(b)
    return (jnp.dot(a1, b1, preferred_element_type=F32)
            + jnp.dot(a1, b2, preferred_element_type=F32)
            + jnp.dot(a2, b1, preferred_element_type=F32))


def _dot_exact_lhs(a_bf16, x):
    out = None
    for p in _split3(x):
        t = jnp.dot(a_bf16, p, preferred_element_type=F32)
        out = t if out is None else out + t
    return out


def _dot_exact_rhs(x, b_bf16):
    out = None
    for p in _split3(x):
        t = jnp.dot(p, b_bf16, preferred_element_type=F32)
        out = t if out is None else out + t
    return out


def _sigmoid(x):
    return 1.0 / (1.0 + jnp.exp(-x))


def _silu(x):
    return x * _sigmoid(x)


def _rms(x, g):
    return x * lax.rsqrt(jnp.mean(x * x, axis=-1, keepdims=True) + EPS) * g


def _ada_body(c_ref, w_ref, b_ref, o_ref):
    o_ref[...] = _dot_f32(_silu(c_ref[...]), w_ref[...]) + b_ref[...]


def _ada(c, w_ada, b_ada):
    n, d = c.shape
    nout = w_ada.shape[1]
    tn = d
    return pl.pallas_call(
        _ada_body,
        grid=(nout // tn,),
        in_specs=[pl.BlockSpec((n, d), lambda j: (0, 0)),
                  pl.BlockSpec((d, tn), lambda j: (0, j)),
                  pl.BlockSpec((1, tn), lambda j: (0, j))],
        out_specs=pl.BlockSpec((n, tn), lambda j: (0, j)),
        out_shape=_sds((n, nout)),
        compiler_params=_params(("arbitrary",)),
        name="ada",
    )(c, w_ada, b_ada.reshape(1, nout))


_C_ZQ = 0
_C_ZK = _C_ZQ + GLA_QK
_C_ZV = _C_ZK + GLA_QK
_C_ZR = _C_ZV + GLA_V
_C_ZQN = _C_ZR + GLA_V
_C_ZKV = _C_ZQN + NSA_Q
_C_MISC = _C_ZKV + NSA_KV
_C_END = _C_MISC + LANES
GATE_COL0 = GLA_RANK


def _inproj_body(x_ref, sc_ref, sh_ref, g_ref, w_ref, wg_ref, bg_ref,
                 qg_ref, kg_ref, vg_ref, rg_ref, qn_ref, rows_ref, winr_ref, logf_ref, gts_ref):
    h = _rms(x_ref[0], g_ref[...]) * (1.0 + sc_ref[0]) + sh_ref[0]
    hb = h.astype(BF16)

    def proj(a, b):
        return jnp.dot(hb, w_ref[:, a:b], preferred_element_type=F32)

    qg_ref[0] = proj(_C_ZQ, _C_ZK) * (GLA_DK ** -0.5)
    kg_ref[0] = proj(_C_ZK, _C_ZV)
    vg_ref[0] = proj(_C_ZV, _C_ZR)
    rg_ref[0] = _silu(proj(_C_ZR, _C_ZQN))
    qn_ref[0] = proj(_C_ZQN, _C_ZKV) * (NSA_DH ** -0.5)
    rows_ref[0] = proj(_C_ZKV, _C_ZKV + ROWS_LANES)
    winr_ref[0] = proj(_C_ZKV + ROWS_LANES, _C_MISC)
    zm = proj(_C_MISC, _C_END)
    gts_ref[0] = _sigmoid(zm)
    logit = jnp.dot(zm.astype(BF16), wg_ref[...], preferred_element_type=F32) + bg_ref[...]
    log_sig = jnp.minimum(logit, 0.0) - jnp.log(1.0 + jnp.exp(-jnp.abs(logit)))
    logf_ref[0] = log_sig * (1.0 / GLA_NORMALIZER)


def _mod_spec(mod, tq):
    d = mod.shape[-1]
    if mod.shape[1] == 1:
        return pl.BlockSpec((1, 1, d), lambda b, t: (b, 0, 0))
    return pl.BlockSpec((1, tq, d), lambda b, t: (b, t, 0))


def _inproj(x, sc, sh, gain, w_perm, wg_pad, bg):
    bsz, t, d = x.shape
    tq = min(t, 512)
    assert t % tq == 0
    widths = (GLA_QK, GLA_QK, GLA_V, GLA_V, NSA_Q, ROWS_LANES, NSA_KV - ROWS_LANES, GLA_QK, LANES)
    tok = lambda n: pl.BlockSpec((1, tq, n), lambda b, i: (b, i, 0))
    full = lambda a: pl.BlockSpec(a.shape, lambda b, i: (0,) * a.ndim)
    return pl.pallas_call(
        _inproj_body,
        grid=(bsz, t // tq),
        in_specs=[tok(d), _mod_spec(sc, tq), _mod_spec(sh, tq), full(gain), full(w_perm),
                  full(wg_pad), full(bg)],
        out_specs=[tok(n) for n in widths],
        out_shape=[_sds((bsz, t, n)) for n in widths],
        compiler_params=_params(("arbitrary", "arbitrary")),
        name="inproj",
    )(x, sc, sh, gain, w_perm, wg_pad, bg)


def _gla_body(q_ref, k_ref, v_ref, f_ref, r_ref, s0_ref, gn_ref, o_ref, sfin_ref, s_scr,
              *, rows, nchunks):
    c_len = GLA_CHUNK
    t = pl.program_id(1)

    @pl.when(t == 0)
    def _():
        s_scr[...] = s0_ref[0]

    n = nchunks * c_len

    def load(ref):
        x = ref[0]
        if rows < n:
            x = jnp.concatenate([x, jnp.zeros((n - rows, x.shape[1]), F32)], axis=0)
        return x

    q, k, v, g, r = (load(ref) for ref in (q_ref, k_ref, v_ref, f_ref, r_ref))
    ri = lax.broadcasted_iota(jnp.int32, (n, n), 0)
    ci = lax.broadcasted_iota(jnp.int32, (n, n), 1)
    shift = c_len.bit_length() - 1
    same_chunk = lax.shift_right_logical(ri, shift) == lax.shift_right_logical(ci, shift)
    tril = jnp.where(ci <= ri, jnp.where(same_chunk, 1.0, 0.0), 0.0).astype(BF16)
    b = None
    for gp in _split3(g):
        term = jnp.dot(tril, gp, preferred_element_type=F32)
        b = term if b is None else b + term
    causal = (lax.broadcasted_iota(jnp.int32, (c_len, c_len), 1)
              <= lax.broadcasted_iota(jnp.int32, (c_len, c_len), 0))

    state = [s_scr[h] for h in range(GLA_HEADS)]
    for c in range(nchunks):
        cs = slice(c * c_len, (c + 1) * c_len)
        b_c = b[cs]
        b_last = b_c[c_len - 1:c_len, :]
        decay_t = jnp.exp(jnp.broadcast_to(b_last, (GLA_DV, GLA_QK)).T)
        b_mid = b_c[c_len // 2 - 1:c_len // 2, :]
        q_inter = q[cs] * jnp.exp(b_c)
        q_intra = q[cs] * jnp.exp(b_c - b_mid)
        k_intra = k[cs] * jnp.exp(b_mid - b_c)
        k_out = k[cs] * jnp.exp(b_last - b_c)
        v_c = v[cs]
        outs = []
        for h in range(GLA_HEADS):
            ks = slice(h * GLA_DK, (h + 1) * GLA_DK)
            vs = slice(h * GLA_DV, (h + 1) * GLA_DV)
            a = jnp.where(causal, _dot_nt(q_intra[:, ks], k_intra[:, ks]), 0.0)
            o = _dot(a, v_c[:, vs]) + _dot(q_inter[:, ks], state[h])
            state[h] = state[h] * decay_t[ks, :] + _dot_tn(k_out[:, ks], v_c[:, vs])
            outs.append(_rms(o, gn_ref[...]))
        on = jnp.concatenate(outs, axis=1) * r[cs]
        lo = c * c_len
        o_ref[0, lo:min(lo + c_len, rows), :] = on[:min(c_len, rows - lo)]
    for h in range(GLA_HEADS):
        s_scr[h] = state[h]

    @pl.when(t == pl.num_programs(1) - 1)
    def _():
        sfin_ref[0] = s_scr[...]


def _gla(qg, kg, vg, logf, rg, s0, gnorm):
    bsz, t, _ = qg.shape
    tt = min(t, 4 * GLA_CHUNK)
    assert t % tt == 0 and (tt % GLA_CHUNK == 0 or tt < GLA_CHUNK)
    nchunks = max(tt // GLA_CHUNK, 1)
    tok = lambda n: pl.BlockSpec((1, tt, n), lambda b, i: (b, i, 0))
    st = pl.BlockSpec((1, GLA_HEADS, GLA_DK, GLA_DV), lambda b, i: (b, 0, 0, 0))
    return pl.pallas_call(
        functools.partial(_gla_body, rows=tt, nchunks=nchunks),
        grid=(bsz, t // tt),
        in_specs=[tok(GLA_QK), tok(GLA_QK), tok(GLA_V), tok(GLA_QK), tok(GLA_V), st,
                  pl.BlockSpec((1, GLA_DV), lambda b, i: (0, 0))],
        out_specs=[tok(GLA_V), st],
        out_shape=[_sds((bsz, t, GLA_V)), _sds((bsz, GLA_HEADS, GLA_DK, GLA_DV))],
        scratch_shapes=[pltpu.VMEM((GLA_HEADS, GLA_DK, GLA_DV), F32)],
        compiler_params=_params(("arbitrary", "arbitrary")),
        name="gla",
    )(qg, kg, vg, logf, rg, s0, gnorm)


def _posbias_body(p_ref, w_ref, o_ref):
    o_ref[0, 0] = _dot_f32(p_ref[0, 0], w_ref[0, 0])


def _posbias(cmp_pos, cmp_w1):
    kdim = CMP_STRIDE * NSA_DH
    pos = jnp.broadcast_to(cmp_pos.reshape(2, 2, 1, kdim), (2, 2, 8, kdim))
    w = cmp_w1.reshape(2, 2, kdim, CMP_HIDDEN)
    return pl.pallas_call(
        _posbias_body,
        grid=(2, 2),
        in_specs=[pl.BlockSpec((1, 1, 8, kdim), lambda a, b: (a, b, 0, 0)),
                  pl.BlockSpec((1, 1, kdim, CMP_HIDDEN), lambda a, b: (a, b, 0, 0))],
        out_specs=pl.BlockSpec((1, 1, 8, CMP_HIDDEN), lambda a, b: (a, b, 0, 0)),
        out_shape=_sds((2, 2, 8, CMP_HIDDEN)),
        compiler_params=_params(("arbitrary", "arbitrary")),
        name="posbias",
    )(pos, w)


def _compress_rows(x_ref, w1_ref, pb_ref, b1_ref, w2_ref, b2_ref):
    nch = x_ref.shape[0] // CMP_STRIDE
    lo = lax.broadcasted_iota(jnp.int32, (nch, KV_LANES), 1) < NSA_DH
    cols = [[], []]
    for jp in range(CMP_STRIDE // 2):
        xe = x_ref[pl.ds(2 * jp, nch, stride=CMP_STRIDE), :]
        xo = x_ref[pl.ds(2 * jp + 1, nch, stride=CMP_STRIDE), :]
        cols[0].append(jnp.where(lo, xe, pltpu.roll(xo, NSA_DH, axis=1)).astype(BF16))
        cols[1].append(jnp.where(lo, pltpu.roll(xe, NSA_DH, axis=1), xo).astype(BF16))
    bias = pb_ref[0, 0, 0:1, :] + pb_ref[0, 1, 0:1, :] + b1_ref[0]
    outs = []
    for h in range(NSA_KVH):
        z = jnp.dot(jnp.concatenate(cols[h], axis=1), w1_ref[0], preferred_element_type=F32)
        z_second = z[:, CMP_HIDDEN:2 * CMP_HIDDEN]
        z_next = jnp.concatenate([z_second[1:], jnp.zeros((1, CMP_HIDDEN), F32)], axis=0)
        pre = z[:, 0:CMP_HIDDEN] + z_next + bias
        outs.append(_dot(_silu(pre), w2_ref[0]) + b2_ref[0])
    return jnp.concatenate(outs, axis=1)


def _compress_weights(cmp_w1, cmp_w2):
    kdim = CMP_STRIDE * NSA_DH
    w1 = jnp.concatenate([cmp_w1[:, :CMP_STRIDE].reshape(2, kdim, CMP_HIDDEN),
                          cmp_w1[:, CMP_STRIDE:].reshape(2, kdim, CMP_HIDDEN)], axis=-1)
    return w1.astype(BF16), cmp_w2.astype(BF16)


def _compress_wspec(shape):
    nd = len(shape)
    return pl.BlockSpec((1,) + shape[1:], lambda b, cc, *_: (cc,) + (0,) * (nd - 1))


def _compress_prompt_body(x_ref, w1_ref, pb_ref, b1_ref, w2_ref, b2_ref, o_ref):
    o_ref[0, 0] = _compress_rows(x_ref.at[0], w1_ref, pb_ref, b1_ref, w2_ref, b2_ref)


def _compress_prompt(rows, w1, posb, b1, w2, b2):
    bsz, t, _ = rows.shape
    assert t % LANES == 0
    nch = t // CMP_STRIDE
    weights = (w1, posb, b1, w2, b2)
    return pl.pallas_call(
        _compress_prompt_body,
        grid=(bsz, 2),
        in_specs=[pl.BlockSpec((1, t, KV_LANES), lambda b, cc: (b, 0, cc))]
        + [_compress_wspec(a.shape) for a in weights],
        out_specs=pl.BlockSpec((1, 1, nch, KV_LANES), lambda b, cc: (cc, b, 0, 0)),
        out_shape=_sds((2, bsz, nch, KV_LANES)),
        compiler_params=_params(("arbitrary", "arbitrary")),
        name="compress_prompt",
    )(rows, *weights)


def _compress_paged_body(pt_ref, cache_ref, w1_ref, pb_ref, b1_ref, w2_ref, b2_ref, o_ref,
                         xt_buf, x_scr, sem, *, npages):
    b = pl.program_id(0)
    cc = pl.program_id(1)
    step = b * 2 + cc
    nsteps = pl.num_programs(0) * 2
    page = xt_buf.shape[-1]

    def page_copy(bb, c, slot, p):
        return pltpu.make_async_copy(cache_ref.at[pt_ref[bb, p], c], xt_buf.at[slot, p],
                                     sem.at[slot])

    @pl.when(step == 0)
    def _():
        for p in range(npages):
            page_copy(0, 0, 0, p).start()

    @pl.when(step + 1 < nsteps)
    def _():
        nxt = step + 1
        for p in range(npages):
            page_copy(nxt // 2, nxt % 2, nxt % 2, p).start()

    slot = step % 2
    for p in range(npages):
        page_copy(b, cc, slot, p).wait()
    for p in range(npages):
        x_scr[p * page:(p + 1) * page, :] = xt_buf[slot, p].T
    o_ref[0, 0] = _compress_rows(x_scr, w1_ref, pb_ref, b1_ref, w2_ref, b2_ref)


def _compress_paged(cache_t, page_table, w1, posb, b1, w2, b2):
    bsz, npages = page_table.shape
    page = cache_t.shape[-1]
    assert page == LANES and cache_t.shape[2] == KV_LANES
    nch = npages * page // CMP_STRIDE
    weights = (w1, posb, b1, w2, b2)
    grid_spec = pltpu.PrefetchScalarGridSpec(
        num_scalar_prefetch=1,
        grid=(bsz, 2),
        in_specs=[pl.BlockSpec(memory_space=pl.ANY)] + [_compress_wspec(a.shape) for a in weights],
        out_specs=pl.BlockSpec((1, 1, nch, KV_LANES), lambda b, cc, pt: (cc, b, 0, 0)),
        scratch_shapes=[pltpu.VMEM((2, npages, KV_LANES, page), F32),
                        pltpu.VMEM((npages * page, KV_LANES), F32),
                        pltpu.SemaphoreType.DMA((2,))],
    )
    return pl.pallas_call(
        functools.partial(_compress_paged_body, npages=npages),
        grid_spec=grid_spec,
        out_shape=_sds((2, bsz, nch, KV_LANES)),
        compiler_params=_params(("arbitrary", "arbitrary")),
        name="compress_paged",
    )(page_table, cache_t, *weights)


def _cmp_to_slc_map(nc, ns):
    start = np.arange(nc) * CMP_STRIDE
    bs = np.arange(ns) * SLC_BLOCK
    ov = (np.minimum(start[:, None] + CMP_LEN, bs[None, :] + SLC_BLOCK)
          - np.maximum(start[:, None], bs[None, :]))
    return (np.clip(ov, 0, None) / CMP_LEN).astype(np.float32)


def _select_top(score_t, ns, n_sel):
    nrow, nq = score_t.shape
    groups = [score_t[8 * v:8 * v + 8, :] for v in range(nrow // 8)]
    ranks = [jnp.zeros((8, nq), F32) for _ in groups]
    sub = lax.broadcasted_iota(jnp.int32, (8, nq), 0)
    for i in range(ns):
        row = jnp.broadcast_to(score_t[i:i + 1, :], (8, nq))
        for v, grp in enumerate(groups):
            if i < 8 * v:
                ranks[v] = jnp.where(row >= grp, ranks[v] + 1.0, ranks[v])
            elif i >= 8 * v + 8:
                ranks[v] = jnp.where(row > grp, ranks[v] + 1.0, ranks[v])
            else:
                later = jnp.where(row >= grp, 1.0, 0.0)
                earlier = jnp.where(row > grp, 1.0, 0.0)
                ranks[v] = ranks[v] + jnp.where(sub > (i - 8 * v), later, earlier)
    rank = jnp.concatenate(ranks, axis=0)
    return jnp.where(rank < float(n_sel), 1.0, 0.0)


def _nsa_prompt_body(q_ref, kc_ref, vc_ref, ks_ref, vs_ref, kw_ref, vw_ref, g_ref,
                     slope_ref, dq_ref, slope_c_ref, dc_ref, mmap_ref, kfeat_ref,
                     o_ref, qs_scr, qw_scr, m_scr, acc_scr, oc_scr, *, tq, ns, n_sel):
    i = pl.program_id(1)
    t0 = i * tq
    tk = tq
    m_rows = NSA_G * tq
    nwin = WINDOW // tk
    lane_q = lax.broadcasted_iota(jnp.int32, (tq, KV_LANES), 1)
    lane_m = lax.broadcasted_iota(jnp.int32, (m_rows, LANES), 1)
    lane_v = lax.broadcasted_iota(jnp.int32, (tk, KV_LANES), 1)
    dq = dq_ref[...]
    rel_blk = ((lane_m - t0 // SLC_BLOCK) * SLC_BLOCK).astype(F32)
    SEL, WIN = 0, 1

    def tile(br, h, j, k_ref, v_ref, q_scr, mask, pen):
        k0 = pl.multiple_of(j * tk, tk)
        kt = jnp.concatenate([k_ref[0, pl.ds(k0, tk), :].astype(BF16),
                              kfeat_ref[pl.ds(k0, tk), :]], axis=1)
        s = lax.dot_general(q_scr[h], kt, (((1,), (1,)), ((), ())), preferred_element_type=F32)
        if mask == "causal":
            s = jnp.where(dq >= 0.0, s, NEG)
        elif mask == "far":
            s = jnp.where(dq <= 0.0, s, NEG)
        if pen is not None:
            s = s + pen
        vt = jnp.where((lane_v // NSA_DH) == h, v_ref[0, pl.ds(k0, tk), :], 1.0).astype(BF16)
        m_old = m_scr[br, h]
        m_new = jnp.maximum(m_old, jnp.max(s, axis=-1, keepdims=True))
        p = jnp.exp(s - jnp.concatenate([m_new] * (tk // LANES), axis=1))
        acc_scr[br, h] = (jnp.exp(m_old - m_new) * acc_scr[br, h]
                          + jnp.dot(p.astype(BF16), vt, preferred_element_type=F32))
        m_scr[br, h] = m_new

    def finish(br, h):
        acc = acc_scr[br, h]
        return acc * (1.0 / pltpu.roll(acc, NSA_DH, axis=1))

    m_scr[...] = jnp.full(m_scr.shape, NEG, F32)
    acc_scr[...] = jnp.zeros(acc_scr.shape, F32)

    qbs = []
    for h in range(NSA_KVH):
        qs = []
        for g in range(NSA_G):
            c0 = (h * NSA_G + g) * NSA_DH
            qg = q_ref[0, :, c0:c0 + NSA_DH]
            qq = jnp.concatenate([qg, qg], axis=1)
            qs.append(jnp.where((lane_q // NSA_DH) == h, qq, 0.0))
        qbs.append(jnp.concatenate(qs, axis=0).astype(BF16))
        slope = slope_ref[h]
        feat_w = jnp.where(lane_m < ns, slope * rel_blk, jnp.where(lane_m == ns, slope, 0.0))
        qw_scr[h] = jnp.concatenate([qbs[h], feat_w.astype(BF16)], axis=1)

    for d in range(nwin + 1):
        pen = None if d == 0 else jnp.where(i >= d, 0.0, NEG)
        mask = "causal" if d == 0 else ("far" if d == nwin else None)
        for h in range(NSA_KVH):
            tile(WIN, h, jnp.maximum(i - d, 0), kw_ref, vw_ref, qw_scr, mask, pen)

    for h in range(NSA_KVH):
        qb = qbs[h]

        d_c = dc_ref[...] + t0.astype(F32)
        s_c = _dot_nt(qb, kc_ref[0, 0])
        s_c = jnp.where(d_c >= 0.0, s_c - slope_c_ref[h] * d_c, NEG)
        mx = jnp.max(s_c, axis=-1, keepdims=True)
        p = jnp.where(s_c > 0.5 * NEG, jnp.exp(s_c - mx), 0.0)
        p_c = p * (1.0 / jnp.maximum(jnp.sum(p, axis=-1, keepdims=True), 1e-30))
        oc_scr[h] = _dot(p_c, vc_ref[0, 0])

        p_sum = p_c[0:tq]
        for g in range(1, NSA_G):
            p_sum = p_sum + p_c[g * tq:(g + 1) * tq]
        imp = _dot_exact_rhs(p_sum, mmap_ref[...])
        blk = lax.broadcasted_iota(jnp.int32, (tq, ns), 1)
        cur = (t0 + lax.broadcasted_iota(jnp.int32, (tq, ns), 0)) // SLC_BLOCK
        forced = (blk == 0) | (blk == cur) | (blk == cur - 1)
        score = jnp.where(blk <= cur, jnp.where(forced, SEL_FORCE, imp), -SEL_FORCE)
        score = jnp.concatenate([score, jnp.full((tq, LANES - ns), PAD_SCORE, F32)], axis=1)
        ns8 = -(-ns // 8) * 8
        sel_t = _select_top(score.T[0:ns8], ns, n_sel)
        sel = jnp.concatenate([sel_t, jnp.zeros((LANES - ns8, tq), F32)], axis=0).T
        sel = jnp.concatenate([sel] * NSA_G, axis=0)

        feat_w = qw_scr[h, :, LANES:2 * LANES].astype(F32)
        feat_s = jnp.where(lane_m < ns, jnp.where(sel > 0.5, feat_w, -MASK_BIAS), feat_w)
        qs_scr[h] = jnp.concatenate([qb, feat_s.astype(BF16)], axis=1)

    for h in range(NSA_KVH):
        tile(SEL, h, i, ks_ref, vs_ref, qs_scr, "causal", None)

    def sel_body(jj, carry):
        ja = i - 1 - 2 * jj
        jb = ja - 1
        pen_b = jnp.where(jb >= 0, 0.0, NEG)
        for h in range(NSA_KVH):
            tile(SEL, h, ja, ks_ref, vs_ref, qs_scr, None, None)
        for h in range(NSA_KVH):
            tile(SEL, h, jnp.maximum(jb, 0), ks_ref, vs_ref, qs_scr, None, pen_b)
        return carry

    lax.fori_loop(0, (i + 1) // 2, sel_body, 0)
    o_s = [finish(SEL, h) for h in range(NSA_KVH)]
    o_w = [finish(WIN, h) for h in range(NSA_KVH)]

    gates = g_ref[0]
    out_cols = []
    for h in range(NSA_KVH):
        o_c = oc_scr[h]
        for g in range(NSA_G):
            col = GATE_COL0 + (h * NSA_G + g) * 3
            rs = slice(g * tq, (g + 1) * tq)
            hs = slice(h * NSA_DH, (h + 1) * NSA_DH)
            out_cols.append(gates[:, col:col + 1] * o_c[rs, hs]
                            + gates[:, col + 1:col + 2] * o_s[h][rs, hs]
                            + gates[:, col + 2:col + 3] * o_w[h][rs, hs])
    o_ref[0] = jnp.concatenate(out_cols, axis=1)


def _alibi_slopes():
    h = np.arange(1, NSA_HEADS + 1, dtype=np.float32)
    return np.exp2(-8.0 * h / NSA_HEADS).astype(np.float32).reshape(NSA_KVH, NSA_G)


def _nsa_prompt(qn, kcvc, rows, winr, gts):
    bsz, t, _ = qn.shape
    tq = min(t, 256)
    assert t % tq == 0 and WINDOW % tq == 0 and tq % SLC_BLOCK == 0 and tq % LANES == 0
    nch = kcvc.shape[2]
    ns = -(-t // SLC_BLOCK)
    assert ns < LANES
    n_sel = min(SLC_TOPN, ns)
    m_rows = NSA_G * tq
    slopes = _alibi_slopes()
    qi = np.tile(np.arange(tq, dtype=np.float32), NSA_G)
    slope_rows = np.repeat(slopes, tq, axis=1)
    slope_mat = np.ascontiguousarray(np.broadcast_to(slope_rows[:, :, None], (NSA_KVH, m_rows, LANES)))
    dq_mat = qi[:, None] - np.arange(tq, dtype=np.float32)[None, :]
    slope_c = np.ascontiguousarray(np.broadcast_to(slope_rows[:, :, None], (NSA_KVH, m_rows, nch)))
    cmp_end = np.arange(nch, dtype=np.float32) * CMP_STRIDE + (CMP_LEN - 1)
    dc_mat = qi[:, None] - cmp_end[None, :]
    mmap = jnp.asarray(_cmp_to_slc_map(nch, ns), BF16)
    key = np.arange(t)
    kfeat = np.zeros((t, LANES), np.float32)
    kfeat[key, key // SLC_BLOCK] = 1.0
    kfeat[:, ns] = key % SLC_BLOCK

    kv = lambda c: pl.BlockSpec((1, t, KV_LANES), lambda b, i: (b, 0, c))
    cst = lambda a: pl.BlockSpec(a.shape, lambda b, i: (0,) * a.ndim)
    consts = [jnp.asarray(slope_mat), jnp.asarray(dq_mat), jnp.asarray(slope_c),
              jnp.asarray(dc_mat), mmap, jnp.asarray(kfeat, BF16)]
    return pl.pallas_call(
        functools.partial(_nsa_prompt_body, tq=tq, ns=ns, n_sel=n_sel),
        grid=(bsz, t // tq),
        in_specs=[pl.BlockSpec((1, tq, NSA_Q), lambda b, i: (b, i, 0)),
                  pl.BlockSpec((1, 1, nch, KV_LANES), lambda b, i: (0, b, 0, 0)),
                  pl.BlockSpec((1, 1, nch, KV_LANES), lambda b, i: (1, b, 0, 0)),
                  kv(2), kv(3), kv(0), kv(1),
                  pl.BlockSpec((1, tq, LANES), lambda b, i: (b, i, 0))] + [cst(a) for a in consts],
        out_specs=pl.BlockSpec((1, tq, NSA_Q), lambda b, i: (b, i, 0)),
        out_shape=_sds((bsz, t, NSA_Q)),
        scratch_shapes=[pltpu.VMEM((NSA_KVH, m_rows, 2 * LANES), BF16),
                        pltpu.VMEM((NSA_KVH, m_rows, 2 * LANES), BF16),
                        pltpu.VMEM((2, NSA_KVH, m_rows, LANES), F32),
                        pltpu.VMEM((2, NSA_KVH, m_rows, KV_LANES), F32),
                        pltpu.VMEM((NSA_KVH, m_rows, KV_LANES), F32)],
        compiler_params=_params(("arbitrary", "arbitrary")),
        name="nsa_prompt",
    )(qn, kcvc, kcvc, rows, rows, winr, winr, gts, *consts)


def _nsa_sample_body(pt_ref, q_ref, kc_ref, vc_ref, rows_ref, winr_ref, swt_ref, g_ref,
                     slope_ref, pos_ref, cur_ref, dc_ref, gsum_ref, mmap_ref, expand_ref,
                     cache_ref, o_ref, wout_ref, kvbuf, sem, *, past, t, npages, ns, n_sel):
    b = pl.program_id(0)
    nb = pl.num_programs(0)
    page = past // npages
    m_rows = NSA_HEADS * t
    half = m_rows // NSA_KVH
    pw = swt_ref.shape[-1]

    def page_copy(bb, slot, p):
        return pltpu.make_async_copy(cache_ref.at[pt_ref[bb, p], pl.ds(2, 2)],
                                     kvbuf.at[slot, :, :, pl.ds(p * page, page)], sem.at[slot])

    @pl.when(b == 0)
    def _():
        for p in range(npages):
            page_copy(0, 0, p).start()

    @pl.when(b + 1 < nb)
    def _():
        for p in range(npages):
            page_copy(b + 1, (b + 1) % 2, p).start()

    pieces = []
    for hh in range(NSA_HEADS):
        qg = q_ref[0, :, hh * NSA_DH:(hh + 1) * NSA_DH]
        z = jnp.zeros_like(qg)
        pieces.append(jnp.concatenate([qg, z] if hh < NSA_G else [z, qg], axis=1))
    qbd = jnp.concatenate(pieces, axis=0).astype(BF16)
    slope = slope_ref[...]
    pos = pos_ref[...]

    def widen(x, n):
        return jnp.concatenate([x] * (n // LANES), axis=1)

    def softmax(parts):
        mx = None
        for s in parts:
            r = jnp.max(s, axis=-1, keepdims=True)
            mx = r if mx is None else jnp.maximum(mx, r)
        ps = [jnp.where(s > 0.5 * NEG, jnp.exp(s - mx), 0.0) for s in parts]
        tot = None
        for p in ps:
            r = jnp.sum(p, axis=-1, keepdims=True)
            tot = r if tot is None else tot + r
        inv = 1.0 / jnp.maximum(tot, 1e-30)
        return [p * inv for p in ps]

    d_c = dc_ref[...]
    nch = d_c.shape[1]
    s_c = _dot_nt(qbd, kc_ref[0, 0])
    (p_c,) = softmax([jnp.where(d_c >= 0.0, s_c - widen(slope, nch) * d_c, NEG)])
    o_c = _dot(p_c, vc_ref[0, 0])

    imp = _dot_exact_lhs(gsum_ref[...], _dot_exact_rhs(p_c, mmap_ref[...]))
    ns8 = imp.shape[1]
    imp = jnp.concatenate([imp, jnp.zeros((m_rows, 2 * LANES - ns8), F32)], axis=1)
    imp = jnp.concatenate([imp, jnp.zeros((LANES - m_rows, 2 * LANES), F32)], axis=0)
    imp_t = imp.T[0:ns8, 0:m_rows]
    blk = lax.broadcasted_iota(jnp.int32, (ns8, m_rows), 0)
    cur = cur_ref[...]
    forced = (blk == 0) | (blk == cur) | (blk == cur - 1)
    score = jnp.where(blk <= cur, jnp.where(forced, SEL_FORCE, imp_t), -SEL_FORCE)
    score = jnp.where(blk < ns, score, PAD_SCORE)
    sel_t = _select_top(score, ns, n_sel)
    sel_t = jnp.concatenate([sel_t, jnp.zeros((ns8, LANES - m_rows), F32)], axis=1)
    sel_t = jnp.concatenate([sel_t, jnp.zeros((2 * LANES - ns8, LANES), F32)], axis=0)
    sel = sel_t.T[0:m_rows, 0:ns8]

    new_ks = rows_ref[0, :, 2 * KV_LANES:3 * KV_LANES]
    new_vs = rows_ref[0, :, 3 * KV_LANES:4 * KV_LANES]
    new_kw = winr_ref[0, :, 0:KV_LANES]
    new_vw = winr_ref[0, :, KV_LANES:2 * KV_LANES]
    dist_n = pos[:, 0:t] - (past + lax.broadcasted_iota(jnp.int32, (m_rows, t), 1)).astype(F32)
    slope_n = slope[:, 0:t]

    kwt = swt_ref[0, 0]
    vwt = swt_ref[0, 1]
    dist_w = widen(pos, pw) - (lax.broadcasted_iota(jnp.int32, (m_rows, pw), 1)
                               + (past - pw)).astype(F32)
    s_w = _dot(qbd, kwt)
    s_w = jnp.where(dist_w >= 0.0,
                    jnp.where(dist_w <= float(WINDOW), s_w - widen(slope, pw) * dist_w, NEG), NEG)
    s_wn = _dot_nt(qbd, new_kw)
    s_wn = jnp.where(dist_n >= 0.0,
                     jnp.where(dist_n <= float(WINDOW), s_wn - slope_n * dist_n, NEG), NEG)
    p_w, p_wn = softmax([s_w, s_wn])
    o_w = _dot_nt(p_w, vwt) + _dot(p_wn, new_vw)
    w_rows = jnp.concatenate([jnp.concatenate([kwt.T, vwt.T], axis=1),
                              winr_ref[0]], axis=0)
    wout_ref[0] = w_rows[pw + t - wout_ref.shape[1]:, :]

    slot = b % 2
    for p in range(npages):
        page_copy(b, slot, p).wait()
    nfull = past // SLC_BLOCK
    mask = jnp.dot(sel.astype(BF16), expand_ref[...], preferred_element_type=F32)
    dist = widen(pos, past) - lax.broadcasted_iota(jnp.int32, (m_rows, past), 1).astype(F32)
    s_s = _dot(qbd, kvbuf[slot, 0])
    s_s = jnp.where(mask > 0.5, jnp.where(dist >= 0.0, s_s - widen(slope, past) * dist, NEG), NEG)
    s_sn = _dot_nt(qbd, new_ks)
    s_sn = jnp.where(sel[:, nfull:nfull + 1] > 0.5,
                     jnp.where(dist_n >= 0.0, s_sn - slope_n * dist_n, NEG), NEG)
    p_s, p_sn = softmax([s_s, s_sn])
    o_s = _dot_nt(p_s, kvbuf[slot, 1]) + _dot(p_sn, new_vs)

    def own(x):
        return jnp.concatenate([x[0:half, 0:NSA_DH], x[half:, NSA_DH:2 * NSA_DH]], axis=0)

    gates = g_ref[0]

    def gate_col(br):
        return jnp.concatenate(
            [gates[:, GATE_COL0 + hh * 3 + br:GATE_COL0 + hh * 3 + br + 1] for hh in range(NSA_HEADS)],
            axis=0)

    mix = gate_col(0) * own(o_c) + gate_col(1) * own(o_s) + gate_col(2) * own(o_w)
    o_ref[0] = jnp.concatenate([mix[hh * t:(hh + 1) * t, :] for hh in range(NSA_HEADS)], axis=1)


def _nsa_sample(qn, kcvc, rows, winr, state_win_t, gts, cache_t, page_table):
    bsz, t, _ = qn.shape
    npages = page_table.shape[1]
    page = cache_t.shape[-1]
    past = npages * page
    pw = state_win_t.shape[-1]
    nch = kcvc.shape[2]
    ns = -(-(past + t) // SLC_BLOCK)
    ns8 = -(-ns // 8) * 8
    m_rows = NSA_HEADS * t
    assert past % SLC_BLOCK == 0 and t <= SLC_BLOCK and t % 8 == 0 and pw + t >= WINDOW
    assert m_rows <= LANES and ns8 <= 2 * LANES and nch % LANES == 0 and pw % LANES == 0
    n_sel = min(SLC_TOPN, ns)
    slopes = _alibi_slopes().reshape(-1)
    lanes1 = np.ones((1, LANES), np.float32)
    slope_b = np.repeat(slopes, t)[:, None].astype(np.float32) * lanes1
    pos_i = np.tile(past + np.arange(t), NSA_HEADS)
    cmp_end = np.arange(nch, dtype=np.float32) * CMP_STRIDE + (CMP_LEN - 1)
    dc_mat = pos_i[:, None].astype(np.float32) - cmp_end[None, :]
    row_kvh = np.arange(m_rows) // (NSA_G * t)
    row_tok = np.arange(m_rows) % t
    gsum = ((row_kvh[:, None] == row_kvh[None, :]) & (row_tok[:, None] == row_tok[None, :]))
    mmap = np.zeros((nch, ns8), np.float32)
    mmap[:, :ns] = _cmp_to_slc_map(nch, ns)
    expand = (np.arange(ns8)[:, None] == (np.arange(past) // SLC_BLOCK)[None, :])
    consts = [jnp.asarray(slope_b), jnp.asarray(pos_i[:, None].astype(np.float32) * lanes1),
              jnp.asarray((pos_i // SLC_BLOCK)[None, :].astype(np.int32)), jnp.asarray(dc_mat),
              jnp.asarray(gsum.astype(np.float32), BF16), jnp.asarray(mmap, BF16),
              jnp.asarray(expand.astype(np.float32), BF16)]

    cst = lambda a: pl.BlockSpec(a.shape, lambda b, pt: (0,) * a.ndim)
    grid_spec = pltpu.PrefetchScalarGridSpec(
        num_scalar_prefetch=1,
        grid=(bsz,),
        in_specs=[pl.BlockSpec((1, t, NSA_Q), lambda b, pt: (b, 0, 0)),
                  pl.BlockSpec((1, 1, nch, KV_LANES), lambda b, pt: (0, b, 0, 0)),
                  pl.BlockSpec((1, 1, nch, KV_LANES), lambda b, pt: (1, b, 0, 0)),
                  pl.BlockSpec((1, t, ROWS_LANES), lambda b, pt: (b, 0, 0)),
                  pl.BlockSpec((1, t, 2 * KV_LANES), lambda b, pt: (b, 0, 0)),
                  pl.BlockSpec((1, 2, KV_LANES, pw), lambda b, pt: (b, 0, 0, 0)),
                  pl.BlockSpec((1, t, LANES), lambda b, pt: (b, 0, 0))]
        + [cst(a) for a in consts] + [pl.BlockSpec(memory_space=pl.ANY)],
        out_specs=[pl.BlockSpec((1, t, NSA_Q), lambda b, pt: (b, 0, 0)),
                   pl.BlockSpec((1, WINDOW, 2 * KV_LANES), lambda b, pt: (b, 0, 0))],
        scratch_shapes=[pltpu.VMEM((2, 2, KV_LANES, past), F32), pltpu.SemaphoreType.DMA((2,))],
    )
    return pl.pallas_call(
        functools.partial(_nsa_sample_body, past=past, t=t, npages=npages, ns=ns, n_sel=n_sel),
        grid_spec=grid_spec,
        out_shape=[_sds((bsz, t, NSA_Q)), _sds((bsz, WINDOW, 2 * KV_LANES))],
        compiler_params=_params(("arbitrary",)),
        name="nsa_sample",
    )(page_table, qn, kcvc, kcvc, rows, winr, state_win_t, gts, *consts, cache_t)


def _ffn_body(x_ref, og_ref, on_ref, gtm_ref, scf_ref, shf_ref, gtf_ref,
              nmp_ref, nfp_ref, nfo_ref, wo_ref, wu_ref, wd_ref, y_ref, *, ff_chunk):
    mix = (jnp.dot(og_ref[0].astype(BF16), wo_ref[0:GLA_V, :], preferred_element_type=F32)
           + jnp.dot(on_ref[0].astype(BF16), wo_ref[GLA_V:GLA_V + NSA_Q, :],
                     preferred_element_type=F32))
    x1 = x_ref[0] + gtm_ref[0] * _rms(mix, nmp_ref[...])
    hb = (_rms(x1, nfp_ref[...]) * (1.0 + scf_ref[0]) + shf_ref[0]).astype(BF16)
    f = None
    for c in range(wu_ref.shape[1] // ff_chunk):
        cs = slice(c * ff_chunk, (c + 1) * ff_chunk)
        u = jnp.maximum(jnp.dot(hb, wu_ref[:, cs], preferred_element_type=F32), 0.0)
        term = jnp.dot((u * u).astype(BF16), wd_ref[cs, :], preferred_element_type=F32)
        f = term if f is None else f + term
    y_ref[0] = x1 + gtf_ref[0] * _rms(f, nfo_ref[...])


def _ffn(x, og, on, gtm, scf, shf, gtf, nmp, nfp, nfo, wo, wu, wd):
    bsz, t, d = x.shape
    tq = min(t, 512)
    assert t % tq == 0
    tok = lambda n: pl.BlockSpec((1, tq, n), lambda b, i: (b, i, 0))
    full = lambda a: pl.BlockSpec(a.shape, lambda b, i: (0,) * a.ndim,
                                  pipeline_mode=pl.Buffered(1))
    vec = lambda a: pl.BlockSpec(a.shape, lambda b, i: (0,) * a.ndim)
    return pl.pallas_call(
        functools.partial(_ffn_body, ff_chunk=min(wu.shape[1], 1024)),
        grid=(bsz, t // tq),
        in_specs=[tok(d), tok(GLA_V), tok(NSA_Q), _mod_spec(gtm, tq), _mod_spec(scf, tq),
                  _mod_spec(shf, tq), _mod_spec(gtf, tq), vec(nmp), vec(nfp), vec(nfo),
                  full(wo), full(wu), full(wd)],
        out_specs=tok(d),
        out_shape=_sds((bsz, t, d)),
        compiler_params=_params(("arbitrary", "arbitrary")),
        name="ffn",
    )(x, og, on, gtm, scf, shf, gtf, nmp, nfp, nfo, wo, wu, wd)


def _permute_w_in(w_in):
    o_zq = 0
    o_zk = o_zq + GLA_QK
    o_zv = o_zk + GLA_QK
    o_za = o_zv + GLA_V
    o_zr = o_za + GLA_RANK
    o_zqn = o_zr + GLA_V
    o_zkv = o_zqn + NSA_Q
    o_zg = o_zkv + NSA_KV
    o_end = o_zg + NSA_GATE
    d = w_in.shape[0]
    pad = jnp.zeros((d, LANES - GLA_RANK - NSA_GATE), w_in.dtype)
    cols = [w_in[:, o_zq:o_za], w_in[:, o_zr:o_zg], w_in[:, o_za:o_zr], w_in[:, o_zg:o_end], pad]
    return jnp.concatenate(cols, axis=1).astype(BF16)


def _layer_weights(l, norm_mix_pre, norm_mix_post, norm_ffn_pre, norm_ffn_post, w_in,
                   gla_w_gate, gla_b_gate, gla_norm, cmp_pos, cmp_w1, cmp_b1, cmp_w2, cmp_b2,
                   w_out, w_up, w_down):
    wg_pad = jnp.zeros((LANES, GLA_QK), F32).at[:GLA_RANK].set(gla_w_gate[l]).astype(BF16)
    w1, w2 = _compress_weights(cmp_w1[l], cmp_w2[l])
    return dict(
        nmpre=norm_mix_pre[l][None, :], nmpost=norm_mix_post[l][None, :],
        nfpre=norm_ffn_pre[l][None, :], nfpost=norm_ffn_post[l][None, :],
        w_perm=_permute_w_in(w_in[l]), wg_pad=wg_pad, bg=gla_b_gate[l][None, :],
        gnorm=gla_norm[l][None, :], w1=w1, w2=w2, posb=_posbias(cmp_pos[l], cmp_w1[l]),
        b1=cmp_b1[l][:, None, :], b2=cmp_b2[l][:, None, :],
        wo=w_out[l].astype(BF16), wu=w_up[l].astype(BF16), wd=w_down[l].astype(BF16))


def _split_ada(ada, rows_per_batch):
    parts = jnp.split(ada, 6, axis=-1)
    if rows_per_batch is None:
        return [p[:, None, :] for p in parts]
    return [jnp.repeat(p, rows_per_batch, axis=0)[None] for p in parts]


def _prompt_layer(x, ada, w, wbuf):
    bsz, t, d = x.shape
    sh_m, sc_m, gt_m, sh_f, sc_f, gt_f = _split_ada(ada, None)
    qg, kg, vg, rg, qn, rows, winr, logf, gts = _inproj(x, sc_m, sh_m, w["nmpre"], w["w_perm"],
                                                        w["wg_pad"], w["bg"])
    s0 = jnp.zeros((bsz, GLA_HEADS, GLA_DK, GLA_DV), F32)
    og, s_fin = _gla(qg, kg, vg, logf, rg, s0, w["gnorm"])
    kcvc = _compress_prompt(rows, w["w1"], w["posb"], w["b1"], w["w2"], w["b2"])
    on = _nsa_prompt(qn, kcvc, rows, winr, gts)
    y = _ffn(x, og, on, gt_m, sc_f, sh_f, gt_f, w["nmpost"], w["nfpre"], w["nfpost"],
             w["wo"], w["wu"], w["wd"])
    win = winr.reshape(bsz, t, 2, NSA_KVH, NSA_DH)
    if t < wbuf:
        win = jnp.pad(win, ((0, 0), (wbuf - t, 0), (0, 0), (0, 0), (0, 0)))
    return y, rows.reshape(bsz, t, 4, NSA_KVH, NSA_DH), win[:, -wbuf:], s_fin


def _sample_layer(x, ada, cache, page_table, state_win, state_gla, w):
    bsz, t, d = x.shape
    n = bsz * t
    sh_m, sc_m, gt_m, sh_f, sc_f, gt_f = _split_ada(ada, t)
    outs = _inproj(x.reshape(1, n, d), sc_m, sh_m, w["nmpre"], w["w_perm"], w["wg_pad"], w["bg"])
    qg, kg, vg, rg, qn, rows, winr, logf, gts = [o.reshape(bsz, t, o.shape[-1]) for o in outs]
    og, s_fin = _gla(qg, kg, vg, logf, rg, state_gla, w["gnorm"])
    n_pool, page = cache.shape[0], cache.shape[1]
    cache_t = jnp.transpose(cache, (0, 2, 3, 4, 1)).reshape(n_pool, 4, KV_LANES, page)
    pw = state_win.shape[1]
    state_win_t = jnp.transpose(state_win, (0, 2, 3, 4, 1)).reshape(bsz, 2, KV_LANES, pw)
    kcvc = _compress_paged(cache_t, page_table, w["w1"], w["posb"], w["b1"], w["w2"], w["b2"])
    on, win = _nsa_sample(qn, kcvc, rows, winr, state_win_t, gts, cache_t, page_table)
    y = _ffn(x.reshape(1, n, d), og.reshape(1, n, GLA_V), on.reshape(1, n, NSA_Q),
             gt_m, sc_f, sh_f, gt_f, w["nmpost"], w["nfpre"], w["nfpost"],
             w["wo"], w["wu"], w["wd"])
    return (y.reshape(bsz, t, d), rows.reshape(bsz, t, 4, NSA_KVH, NSA_DH),
            win.reshape(bsz, WINDOW, 2, NSA_KVH, NSA_DH), s_fin)


def kernel(x_prompt, x_sample, cache_kv, state_win, state_gla, page_table, c_prompt, c_sample,
           norm_mix_pre, norm_mix_post, norm_ffn_pre, norm_ffn_post, w_ada, b_ada, w_in,
           gla_w_gate, gla_b_gate, gla_norm, cmp_pos, cmp_w1, cmp_b1, cmp_w2, cmp_b2,
           w_out, w_up, w_down):
    depth = w_in.shape[0]
    bsz = x_prompt.shape[0]
    wbuf = state_win.shape[2]
    assert wbuf == WINDOW
    c_all = jnp.concatenate([c_prompt, c_sample], axis=0)
    y_p, y_s = x_prompt, x_sample
    outs = [[] for _ in range(6)]
    for l in range(depth):
        w = _layer_weights(l, norm_mix_pre, norm_mix_post, norm_ffn_pre, norm_ffn_post, w_in,
                           gla_w_gate, gla_b_gate, gla_norm, cmp_pos, cmp_w1, cmp_b1, cmp_w2,
                           cmp_b2, w_out, w_up, w_down)
        ada = _ada(c_all, w_ada[l], b_ada[l])
        y_p, r_p, w_p, s_p = _prompt_layer(y_p, ada[:bsz], w, wbuf)
        y_s, r_s, w_s, s_s = _sample_layer(y_s, ada[bsz:], cache_kv[l], page_table,
                                           state_win[l], state_gla[l], w)
        for lst, val in zip(outs, (r_p, r_s, w_p, w_s, s_p, s_s)):
            lst.append(val)
    return (y_p, y_s) + tuple(jnp.stack(o) for o in outs)
```

```python
import functools
import math

import numpy as np
import jax
import jax.numpy as jnp
from jax import lax
from jax.experimental import pallas as pl
from jax.experimental.pallas import tpu as pltpu

F32 = jnp.float32
BF16 = jnp.bfloat16

GLA_HEADS = 4
GLA_DK = 64
GLA_DV = 128
GLA_RANK = 16
GLA_NORMALIZER = 16.0
GLA_CHUNK = 64
NSA_HEADS = 8
NSA_DH = 64
NSA_KVH = 2
NSA_G = NSA_HEADS // NSA_KVH
CMP_STRIDE = 16
CMP_LEN = 32
CMP_HIDDEN = 256
SLC_BLOCK = 64
SLC_TOPN = 16
SEL_FORCE = 1.0e4
WINDOW = 512
EPS = 1e-6
NEG = -1.0e30
PAD_SCORE = -3.0e4
MASK_BIAS = 2.0 ** 60
LANES = 128
VMEM_LIMIT = 56 * 1024 * 1024

GLA_QK = GLA_HEADS * GLA_DK
GLA_V = GLA_HEADS * GLA_DV
NSA_Q = NSA_HEADS * NSA_DH
NSA_KV = 6 * NSA_KVH * NSA_DH
NSA_GATE = 3 * NSA_HEADS
KV_LANES = NSA_KVH * NSA_DH
ROWS_LANES = 4 * KV_LANES


def _sds(shape, dtype=F32):
    return jax.ShapeDtypeStruct(shape, dtype)


def _params(sem):
    return pltpu.CompilerParams(dimension_semantics=sem, vmem_limit_bytes=VMEM_LIMIT)


def _dot(a, b):
    return jnp.dot(a.astype(BF16), b.astype(BF16), preferred_element_type=F32)


def _dot_nt(a, b):
    return lax.dot_general(a.astype(BF16), b.astype(BF16), (((1,), (1,)), ((), ())),
                           preferred_element_type=F32)


def _dot_tn(a, b):
    return lax.dot_general(a.astype(BF16), b.astype(BF16), (((0,), (0,)), ((), ())),
                           preferred_element_type=F32)


def _split3(x):
    p1 = x.astype(BF16)
    r1 = x - p1.astype(F32)
    p2 = r1.astype(BF16)
    p3 = (r1 - p2.astype(F32)).astype(BF16)
    return p1, p2, p3


def _dot_f32(a, b):
    a1, a2, _ = _split3(a)
    b1, b2, _ = _split3(b)
    return (jnp.dot(a1, b1, preferred_element_type=F32)
            + jnp.dot(a1, b2, preferred_element_type=F32)
            + jnp.dot(a2, b1, preferred_element_type=F32))


def _dot_exact_lhs(a_bf16, x):
    out = None
    for p in _split3(x):
        t = jnp.dot(a_bf16, p, preferred_element_type=F32)
        out = t if out is None else out + t
    return out


def _dot_exact_rhs(x, b_bf16):
    out = None
    for p in _split3(x):
        t = jnp.dot(p, b_bf16, preferred_element_type=F32)
        out = t if out is None else out + t
    return out


def _sigmoid(x):
    return 1.0 / (1.0 + jnp.exp(-x))


def _silu(x):
    return x * _sigmoid(x)


def _rms(x, g):
    return x * lax.rsqrt(jnp.mean(x * x, axis=-1, keepdims=True) + EPS) * g


def _ada_body(c_ref, w_ref, b_ref, o_ref):
    o_ref[...] = _dot_f32(_silu(c_ref[...]), w_ref[...]) + b_ref[...]


def _ada(c, w_ada, b_ada):
    n, d = c.shape
    nout = w_ada.shape[1]
    tn = d
    return pl.pallas_call(
        _ada_body,
        grid=(nout // tn,),
        in_specs=[pl.BlockSpec((n, d), lambda j: (0, 0)),
                  pl.BlockSpec((d, tn), lambda j: (0, j)),
                  pl.BlockSpec((1, tn), lambda j: (0, j))],
        out_specs=pl.BlockSpec((n, tn), lambda j: (0, j)),
        out_shape=_sds((n, nout)),
        compiler_params=_params(("arbitrary",)),
        name="ada",
    )(c, w_ada, b_ada.reshape(1, nout))


_C_ZQ = 0
_C_ZK = _C_ZQ + GLA_QK
_C_ZV = _C_ZK + GLA_QK
_C_ZR = _C_ZV + GLA_V
_C_ZQN = _C_ZR + GLA_V
_C_ZKV = _C_ZQN + NSA_Q
_C_MISC = _C_ZKV + NSA_KV
_C_END = _C_MISC + LANES
GATE_COL0 = GLA_RANK


def _inproj_body(x_ref, sc_ref, sh_ref, g_ref, w_ref, wg_ref, bg_ref,
                 qg_ref, kg_ref, vg_ref, rg_ref, qn_ref, rows_ref, winr_ref, logf_ref, gts_ref):
    h = _rms(x_ref[0], g_ref[...]) * (1.0 + sc_ref[0]) + sh_ref[0]
    hb = h.astype(BF16)

    def proj(a, b):
        return jnp.dot(hb, w_ref[:, a:b], preferred_element_type=F32)

    qg_ref[0] = proj(_C_ZQ, _C_ZK) * (GLA_DK ** -0.5)
    kg_ref[0] = proj(_C_ZK, _C_ZV)
    vg_ref[0] = proj(_C_ZV, _C_ZR)
    rg_ref[0] = _silu(proj(_C_ZR, _C_ZQN))
    qn_ref[0] = proj(_C_ZQN, _C_ZKV) * (NSA_DH ** -0.5)
    rows_ref[0] = proj(_C_ZKV, _C_ZKV + ROWS_LANES)
    winr_ref[0] = proj(_C_ZKV + ROWS_LANES, _C_MISC)
    zm = proj(_C_MISC, _C_END)
    gts_ref[0] = _sigmoid(zm)
    logit = jnp.dot(zm.astype(BF16), wg_ref[...], preferred_element_type=F32) + bg_ref[...]
    log_sig = jnp.minimum(logit, 0.0) - jnp.log(1.0 + jnp.exp(-jnp.abs(logit)))
    logf_ref[0] = log_sig * (1.0 / GLA_NORMALIZER)


def _mod_spec(mod, tq):
    d = mod.shape[-1]
    if mod.shape[1] == 1:
        return pl.BlockSpec((1, 1, d), lambda b, t: (b, 0, 0))
    return pl.BlockSpec((1, tq, d), lambda b, t: (b, t, 0))


def _inproj(x, sc, sh, gain, w_perm, wg_pad, bg):
    bsz, t, d = x.shape
    tq = min(t, 512)
    assert t % tq == 0
    widths = (GLA_QK, GLA_QK, GLA_V, GLA_V, NSA_Q, ROWS_LANES, NSA_KV - ROWS_LANES, GLA_QK, LANES)
    tok = lambda n: pl.BlockSpec((1, tq, n), lambda b, i: (b, i, 0))
    full = lambda a: pl.BlockSpec(a.shape, lambda b, i: (0,) * a.ndim)
    return pl.pallas_call(
        _inproj_body,
        grid=(bsz, t // tq),
        in_specs=[tok(d), _mod_spec(sc, tq), _mod_spec(sh, tq), full(gain), full(w_perm),
                  full(wg_pad), full(bg)],
        out_specs=[tok(n) for n in widths],
        out_shape=[_sds((bsz, t, n)) for n in widths],
        compiler_params=_params(("arbitrary", "arbitrary")),
        name="inproj",
    )(x, sc, sh, gain, w_perm, wg_pad, bg)


def _gla_body(q_ref, k_ref, v_ref, f_ref, r_ref, s0_ref, gn_ref, o_ref, sfin_ref, s_scr,
              *, rows, nchunks, nb):
    c_len = GLA_CHUNK
    t = pl.program_id(1)

    @pl.when(t == 0)
    def _():
        s_scr[...] = s0_ref[...]

    n = nchunks * c_len

    def load(ref, bb):
        x = ref[bb]
        if rows < n:
            x = jnp.concatenate([x, jnp.zeros((n - rows, x.shape[1]), F32)], axis=0)
        return x

    ri = lax.broadcasted_iota(jnp.int32, (n, n), 0)
    ci = lax.broadcasted_iota(jnp.int32, (n, n), 1)
    shift = c_len.bit_length() - 1
    same_chunk = lax.shift_right_logical(ri, shift) == lax.shift_right_logical(ci, shift)
    tril = jnp.where(ci <= ri, jnp.where(same_chunk, 1.0, 0.0), 0.0).astype(BF16)
    causal = (lax.broadcasted_iota(jnp.int32, (c_len, c_len), 1)
              <= lax.broadcasted_iota(jnp.int32, (c_len, c_len), 0))

    seqs = []
    for bb in range(nb):
        q, k, v, g, r = (load(ref, bb) for ref in (q_ref, k_ref, v_ref, f_ref, r_ref))
        b = None
        for gp in _split3(g):
            term = jnp.dot(tril, gp, preferred_element_type=F32)
            b = term if b is None else b + term
        seqs.append((q, k, v, r, b, [s_scr[bb, h] for h in range(GLA_HEADS)]))

    for c in range(nchunks):
        cs = slice(c * c_len, (c + 1) * c_len)
        lo = c * c_len
        for bb, (q, k, v, r, b, state) in enumerate(seqs):
            b_c = b[cs]
            b_last = b_c[c_len - 1:c_len, :]
            decay_t = jnp.exp(jnp.broadcast_to(b_last, (GLA_DV, GLA_QK)).T)
            b_mid = b_c[c_len // 2 - 1:c_len // 2, :]
            q_inter = q[cs] * jnp.exp(b_c)
            q_intra = q[cs] * jnp.exp(b_c - b_mid)
            k_intra = k[cs] * jnp.exp(b_mid - b_c)
            k_out = k[cs] * jnp.exp(b_last - b_c)
            v_c = v[cs]
            outs = []
            for h in range(GLA_HEADS):
                ks = slice(h * GLA_DK, (h + 1) * GLA_DK)
                vs = slice(h * GLA_DV, (h + 1) * GLA_DV)
                a = jnp.where(causal, _dot_nt(q_intra[:, ks], k_intra[:, ks]), 0.0)
                o = _dot(a, v_c[:, vs]) + _dot(q_inter[:, ks], state[h])
                state[h] = state[h] * decay_t[ks, :] + _dot_tn(k_out[:, ks], v_c[:, vs])
                outs.append(_rms(o, gn_ref[...]))
            on = jnp.concatenate(outs, axis=1) * r[cs]
            o_ref[bb, lo:min(lo + c_len, rows), :] = on[:min(c_len, rows - lo)]
    for bb, seq in enumerate(seqs):
        for h in range(GLA_HEADS):
            s_scr[bb, h] = seq[5][h]

    @pl.when(t == pl.num_programs(1) - 1)
    def _():
        sfin_ref[...] = s_scr[...]


def _gla(qg, kg, vg, logf, rg, s0, gnorm):
    bsz, t, _ = qg.shape
    tt = min(t, 4 * GLA_CHUNK)
    assert t % tt == 0 and (tt % GLA_CHUNK == 0 or tt < GLA_CHUNK)
    nchunks = max(tt // GLA_CHUNK, 1)
    nb = math.gcd(bsz, 2 if nchunks > 1 else 8)
    tok = lambda n: pl.BlockSpec((nb, tt, n), lambda b, i: (b, i, 0))
    st = pl.BlockSpec((nb, GLA_HEADS, GLA_DK, GLA_DV), lambda b, i: (b, 0, 0, 0))
    return pl.pallas_call(
        functools.partial(_gla_body, rows=tt, nchunks=nchunks, nb=nb),
        grid=(bsz // nb, t // tt),
        in_specs=[tok(GLA_QK), tok(GLA_QK), tok(GLA_V), tok(GLA_QK), tok(GLA_V), st,
                  pl.BlockSpec((1, GLA_DV), lambda b, i: (0, 0))],
        out_specs=[tok(GLA_V), st],
        out_shape=[_sds((bsz, t, GLA_V)), _sds((bsz, GLA_HEADS, GLA_DK, GLA_DV))],
        scratch_shapes=[pltpu.VMEM((nb, GLA_HEADS, GLA_DK, GLA_DV), F32)],
        compiler_params=_params(("arbitrary", "arbitrary")),
        name="gla",
    )(qg, kg, vg, logf, rg, s0, gnorm)


def _posbias_body(p_ref, w_ref, o_ref):
    o_ref[0, 0] = _dot_f32(p_ref[0, 0], w_ref[0, 0])


def _posbias(cmp_pos, cmp_w1):
    kdim = CMP_STRIDE * NSA_DH
    pos = jnp.broadcast_to(cmp_pos.reshape(2, 2, 1, kdim), (2, 2, 8, kdim))
    w = cmp_w1.reshape(2, 2, kdim, CMP_HIDDEN)
    return pl.pallas_call(
        _posbias_body,
        grid=(2, 2),
        in_specs=[pl.BlockSpec((1, 1, 8, kdim), lambda a, b: (a, b, 0, 0)),
                  pl.BlockSpec((1, 1, kdim, CMP_HIDDEN), lambda a, b: (a, b, 0, 0))],
        out_specs=pl.BlockSpec((1, 1, 8, CMP_HIDDEN), lambda a, b: (a, b, 0, 0)),
        out_shape=_sds((2, 2, 8, CMP_HIDDEN)),
        compiler_params=_params(("arbitrary", "arbitrary")),
        name="posbias",
    )(pos, w)


def _compress_rows(x_ref, w1_ref, pb_ref, b1_ref, w2_ref, b2_ref):
    nch = x_ref.shape[0] // CMP_STRIDE
    lo = lax.broadcasted_iota(jnp.int32, (nch, KV_LANES), 1) < NSA_DH
    cols = [[], []]
    for jp in range(CMP_STRIDE // 2):
        xe = x_ref[pl.ds(2 * jp, nch, stride=CMP_STRIDE), :]
        xo = x_ref[pl.ds(2 * jp + 1, nch, stride=CMP_STRIDE), :]
        cols[0].append(jnp.where(lo, xe, pltpu.roll(xo, NSA_DH, axis=1)).astype(BF16))
        cols[1].append(jnp.where(lo, pltpu.roll(xe, NSA_DH, axis=1), xo).astype(BF16))
    bias = pb_ref[0, 0, 0:1, :] + pb_ref[0, 1, 0:1, :] + b1_ref[0]
    outs = []
    for h in range(NSA_KVH):
        z = jnp.dot(jnp.concatenate(cols[h], axis=1), w1_ref[0], preferred_element_type=F32)
        z_second = z[:, CMP_HIDDEN:2 * CMP_HIDDEN]
        z_next = jnp.concatenate([z_second[1:], jnp.zeros((1, CMP_HIDDEN), F32)], axis=0)
        pre = z[:, 0:CMP_HIDDEN] + z_next + bias
        outs.append(_dot(_silu(pre), w2_ref[0]) + b2_ref[0])
    return jnp.concatenate(outs, axis=1)


def _compress_weights(cmp_w1, cmp_w2):
    kdim = CMP_STRIDE * NSA_DH
    w1 = jnp.concatenate([cmp_w1[:, :CMP_STRIDE].reshape(2, kdim, CMP_HIDDEN),
                          cmp_w1[:, CMP_STRIDE:].reshape(2, kdim, CMP_HIDDEN)], axis=-1)
    return w1.astype(BF16), cmp_w2.astype(BF16)


def _compress_wspec(shape):
    nd = len(shape)
    return pl.BlockSpec((1,) + shape[1:], lambda b, cc, *_: (cc,) + (0,) * (nd - 1))


def _compress_prompt_body(x_ref, w1_ref, pb_ref, b1_ref, w2_ref, b2_ref, o_ref):
    o_ref[0, 0] = _compress_rows(x_ref.at[0], w1_ref, pb_ref, b1_ref, w2_ref, b2_ref)


def _compress_prompt(rows, w1, posb, b1, w2, b2):
    bsz, t, _ = rows.shape
    assert t % LANES == 0
    nch = t // CMP_STRIDE
    weights = (w1, posb, b1, w2, b2)
    return pl.pallas_call(
        _compress_prompt_body,
        grid=(bsz, 2),
        in_specs=[pl.BlockSpec((1, t, KV_LANES), lambda b, cc: (b, 0, cc))]
        + [_compress_wspec(a.shape) for a in weights],
        out_specs=pl.BlockSpec((1, 1, nch, KV_LANES), lambda b, cc: (cc, b, 0, 0)),
        out_shape=_sds((2, bsz, nch, KV_LANES)),
        compiler_params=_params(("arbitrary", "arbitrary")),
        name="compress_prompt",
    )(rows, *weights)


def _compress_paged_body(pt_ref, cache_ref, w1_ref, pb_ref, b1_ref, w2_ref, b2_ref, o_ref,
                         xt_buf, x_scr, sem, *, npages):
    b = pl.program_id(0)
    cc = pl.program_id(1)
    step = b * 2 + cc
    nsteps = pl.num_programs(0) * 2
    page = xt_buf.shape[-1]

    def page_copy(bb, c, slot, p):
        return pltpu.make_async_copy(cache_ref.at[pt_ref[bb, p], c], xt_buf.at[slot, p],
                                     sem.at[slot])

    @pl.when(step == 0)
    def _():
        for p in range(npages):
            page_copy(0, 0, 0, p).start()

    @pl.when(step + 1 < nsteps)
    def _():
        nxt = step + 1
        for p in range(npages):
            page_copy(nxt // 2, nxt % 2, nxt % 2, p).start()

    slot = step % 2
    for p in range(npages):
        page_copy(b, cc, slot, p).wait()
    for p in range(npages):
        x_scr[p * page:(p + 1) * page, :] = xt_buf[slot, p].T
    o_ref[0, 0] = _compress_rows(x_scr, w1_ref, pb_ref, b1_ref, w2_ref, b2_ref)


def _compress_paged(cache_t, page_table, w1, posb, b1, w2, b2):
    bsz, npages = page_table.shape
    page = cache_t.shape[-1]
    assert page == LANES and cache_t.shape[2] == KV_LANES
    nch = npages * page // CMP_STRIDE
    weights = (w1, posb, b1, w2, b2)
    grid_spec = pltpu.PrefetchScalarGridSpec(
        num_scalar_prefetch=1,
        grid=(bsz, 2),
        in_specs=[pl.BlockSpec(memory_space=pl.ANY)] + [_compress_wspec(a.shape) for a in weights],
        out_specs=pl.BlockSpec((1, 1, nch, KV_LANES), lambda b, cc, pt: (cc, b, 0, 0)),
        scratch_shapes=[pltpu.VMEM((2, npages, KV_LANES, page), F32),
                        pltpu.VMEM((npages * page, KV_LANES), F32),
                        pltpu.SemaphoreType.DMA((2,))],
    )
    return pl.pallas_call(
        functools.partial(_compress_paged_body, npages=npages),
        grid_spec=grid_spec,
        out_shape=_sds((2, bsz, nch, KV_LANES)),
        compiler_params=_params(("arbitrary", "arbitrary")),
        name="compress_paged",
    )(page_table, cache_t, *weights)


def _cmp_to_slc_map(nc, ns):
    start = np.arange(nc) * CMP_STRIDE
    bs = np.arange(ns) * SLC_BLOCK
    ov = (np.minimum(start[:, None] + CMP_LEN, bs[None, :] + SLC_BLOCK)
          - np.maximum(start[:, None], bs[None, :]))
    return (np.clip(ov, 0, None) / CMP_LEN).astype(np.float32)


def _select_top(score_t, ns, n_sel):
    nrow, nq = score_t.shape
    groups = [score_t[8 * v:8 * v + 8, :] for v in range(nrow // 8)]
    ranks = [jnp.zeros((8, nq), F32) for _ in groups]
    sub = lax.broadcasted_iota(jnp.int32, (8, nq), 0)
    for i in range(ns):
        row = jnp.broadcast_to(score_t[i:i + 1, :], (8, nq))
        for v, grp in enumerate(groups):
            if i < 8 * v:
                ranks[v] = jnp.where(row >= grp, ranks[v] + 1.0, ranks[v])
            elif i >= 8 * v + 8:
                ranks[v] = jnp.where(row > grp, ranks[v] + 1.0, ranks[v])
            else:
                later = jnp.where(row >= grp, 1.0, 0.0)
                earlier = jnp.where(row > grp, 1.0, 0.0)
                ranks[v] = ranks[v] + jnp.where(sub > (i - 8 * v), later, earlier)
    rank = jnp.concatenate(ranks, axis=0)
    return jnp.where(rank < float(n_sel), 1.0, 0.0)


def _nsa_prompt_body(q_ref, kc_ref, vc_ref, ks_ref, vs_ref, kw_ref, vw_ref, g_ref,
                     slope_ref, dq_ref, slope_c_ref, dc_ref, mmap_ref, kfeat_ref,
                     o_ref, qs_scr, qw_scr, m_scr, acc_scr, oc_scr, *, tq, ns, n_sel):
    i = pl.program_id(1)
    t0 = i * tq
    tk = tq
    m_rows = NSA_G * tq
    nwin = WINDOW // tk
    lane_q = lax.broadcasted_iota(jnp.int32, (tq, KV_LANES), 1)
    lane_m = lax.broadcasted_iota(jnp.int32, (m_rows, LANES), 1)
    lane_v = lax.broadcasted_iota(jnp.int32, (tk, KV_LANES), 1)
    dq = dq_ref[...]
    rel_blk = ((lane_m - t0 // SLC_BLOCK) * SLC_BLOCK).astype(F32)
    SEL, WIN = 0, 1

    def tile(br, h, j, k_ref, v_ref, q_scr, mask, pen):
        k0 = pl.multiple_of(j * tk, tk)
        kt = jnp.concatenate([k_ref[0, pl.ds(k0, tk), :].astype(BF16),
                              kfeat_ref[pl.ds(k0, tk), :]], axis=1)
        s = lax.dot_general(q_scr[h], kt, (((1,), (1,)), ((), ())), preferred_element_type=F32)
        if mask == "causal":
            s = jnp.where(dq >= 0.0, s, NEG)
        elif mask == "far":
            s = jnp.where(dq <= 0.0, s, NEG)
        if pen is not None:
            s = s + pen
        vt = jnp.where((lane_v // NSA_DH) == h, v_ref[0, pl.ds(k0, tk), :], 1.0).astype(BF16)
        if mask == "causal":
            m_new = jnp.broadcast_to(jnp.max(s, axis=-1, keepdims=True), (m_rows, LANES))
            p = jnp.exp(s - jnp.concatenate([m_new] * (tk // LANES), axis=1))
            acc_scr[br, h] = jnp.dot(p.astype(BF16), vt, preferred_element_type=F32)
        else:
            m_old = m_scr[br, h]
            m_new = jnp.maximum(m_old, jnp.max(s, axis=-1, keepdims=True))
            p = jnp.exp(s - jnp.concatenate([m_new] * (tk // LANES), axis=1))
            acc_scr[br, h] = (jnp.exp(m_old - m_new) * acc_scr[br, h]
                              + jnp.dot(p.astype(BF16), vt, preferred_element_type=F32))
        m_scr[br, h] = m_new

    def finish(br, h):
        acc = acc_scr[br, h]
        return acc * (1.0 / pltpu.roll(acc, NSA_DH, axis=1))

    qbs = []
    for h in range(NSA_KVH):
        qs = []
        for g in range(NSA_G):
            c0 = (h * NSA_G + g) * NSA_DH
            qg = q_ref[0, :, c0:c0 + NSA_DH]
            qq = jnp.concatenate([qg, qg], axis=1)
            qs.append(jnp.where((lane_q // NSA_DH) == h, qq, 0.0))
        qbs.append(jnp.concatenate(qs, axis=0).astype(BF16))
        slope = slope_ref[h]
        feat_w = jnp.where(lane_m < ns, slope * rel_blk, jnp.where(lane_m == ns, slope, 0.0))
        qw_scr[h] = jnp.concatenate([qbs[h], feat_w.astype(BF16)], axis=1)

    for d in range(nwin + 1):
        pen = None if d == 0 else jnp.where(i >= d, 0.0, NEG)
        mask = "causal" if d == 0 else ("far" if d == nwin else None)
        for h in range(NSA_KVH):
            tile(WIN, h, jnp.maximum(i - d, 0), kw_ref, vw_ref, qw_scr, mask, pen)

    lowest = None
    for h in range(NSA_KVH):
        qb = qbs[h]

        d_c = dc_ref[...] + t0.astype(F32)
        s_c = _dot_nt(qb, kc_ref[0, 0])
        s_c = jnp.where(d_c >= 0.0, s_c - slope_c_ref[h] * d_c, NEG)
        mx = jnp.max(s_c, axis=-1, keepdims=True)
        p = jnp.where(s_c > 0.5 * NEG, jnp.exp(s_c - mx), 0.0)
        p_c = p * (1.0 / jnp.maximum(jnp.sum(p, axis=-1, keepdims=True), 1e-30))
        oc_scr[h] = _dot(p_c, vc_ref[0, 0])

        p_sum = p_c[0:tq]
        for g in range(1, NSA_G):
            p_sum = p_sum + p_c[g * tq:(g + 1) * tq]
        imp = _dot_exact_rhs(p_sum, mmap_ref[...])
        blk = lax.broadcasted_iota(jnp.int32, (tq, ns), 1)
        cur = (t0 + lax.broadcasted_iota(jnp.int32, (tq, ns), 0)) // SLC_BLOCK
        forced = (blk == 0) | (blk == cur) | (blk == cur - 1)
        score = jnp.where(blk <= cur, jnp.where(forced, SEL_FORCE, imp), -SEL_FORCE)
        score = jnp.concatenate([score, jnp.full((tq, LANES - ns), PAD_SCORE, F32)], axis=1)
        ns8 = -(-ns // 8) * 8
        sel_t = _select_top(score.T[0:ns8], ns, n_sel)
        blk_t = lax.broadcasted_iota(jnp.int32, (ns8, tq), 0).astype(F32)
        low = jnp.where(sel_t > 0.5, jnp.where(blk_t >= float(tk // SLC_BLOCK), blk_t, float(ns8)),
                        float(ns8))
        lowest = low if lowest is None else jnp.minimum(lowest, low)
        sel = jnp.concatenate([sel_t, jnp.zeros((LANES - ns8, tq), F32)], axis=0).T
        sel = jnp.concatenate([sel] * NSA_G, axis=0)

        feat_w = qw_scr[h, :, LANES:2 * LANES].astype(F32)
        feat_s = jnp.where(lane_m < ns, jnp.where(sel > 0.5, feat_w, -MASK_BIAS), feat_w)
        qs_scr[h] = jnp.concatenate([qb, feat_s.astype(BF16)], axis=1)

    for h in range(NSA_KVH):
        tile(SEL, h, i, ks_ref, vs_ref, qs_scr, "causal", None)

    lo = jnp.minimum((jnp.min(lowest) * (SLC_BLOCK / tk)).astype(jnp.int32), i)
    n_mid = i - lo
    n_tiles = n_mid + jnp.where(i >= 1, 1, 0)

    def sel_body(jj, carry):
        ia = 2 * jj
        ib = ia + 1
        ja = jnp.where(ia < n_mid, i - 1 - ia, 0)
        jb = jnp.where(ib < n_mid, i - 1 - ib, 0)
        pen_b = jnp.where(ib < n_tiles, 0.0, NEG)
        for h in range(NSA_KVH):
            tile(SEL, h, ja, ks_ref, vs_ref, qs_scr, None, None)
        for h in range(NSA_KVH):
            tile(SEL, h, jb, ks_ref, vs_ref, qs_scr, None, pen_b)
        return carry

    lax.fori_loop(0, (n_tiles + 1) // 2, sel_body, 0)
    o_s = [finish(SEL, h) for h in range(NSA_KVH)]
    o_w = [finish(WIN, h) for h in range(NSA_KVH)]

    gates = g_ref[0]
    out_cols = []
    for h in range(NSA_KVH):
        o_c = oc_scr[h]
        for g in range(NSA_G):
            col = GATE_COL0 + (h * NSA_G + g) * 3
            rs = slice(g * tq, (g + 1) * tq)
            hs = slice(h * NSA_DH, (h + 1) * NSA_DH)
            out_cols.append(gates[:, col:col + 1] * o_c[rs, hs]
                            + gates[:, col + 1:col + 2] * o_s[h][rs, hs]
                            + gates[:, col + 2:col + 3] * o_w[h][rs, hs])
    o_ref[0] = jnp.concatenate(out_cols, axis=1)


def _alibi_slopes():
    h = np.arange(1, NSA_HEADS + 1, dtype=np.float32)
    return np.exp2(-8.0 * h / NSA_HEADS).astype(np.float32).reshape(NSA_KVH, NSA_G)


def _nsa_prompt(qn, kcvc, rows, winr, gts):
    bsz, t, _ = qn.shape
    tq = min(t, 256)
    assert t % tq == 0 and WINDOW % tq == 0 and tq % SLC_BLOCK == 0 and tq % LANES == 0
    nch = kcvc.shape[2]
    ns = -(-t // SLC_BLOCK)
    assert ns < LANES
    n_sel = min(SLC_TOPN, ns)
    m_rows = NSA_G * tq
    slopes = _alibi_slopes()
    qi = np.tile(np.arange(tq, dtype=np.float32), NSA_G)
    slope_rows = np.repeat(slopes, tq, axis=1)
    slope_mat = np.ascontiguousarray(np.broadcast_to(slope_rows[:, :, None], (NSA_KVH, m_rows, LANES)))
    dq_mat = qi[:, None] - np.arange(tq, dtype=np.float32)[None, :]
    slope_c = np.ascontiguousarray(np.broadcast_to(slope_rows[:, :, None], (NSA_KVH, m_rows, nch)))
    cmp_end = np.arange(nch, dtype=np.float32) * CMP_STRIDE + (CMP_LEN - 1)
    dc_mat = qi[:, None] - cmp_end[None, :]
    mmap = jnp.asarray(_cmp_to_slc_map(nch, ns), BF16)
    key = np.arange(t)
    kfeat = np.zeros((t, LANES), np.float32)
    kfeat[key, key // SLC_BLOCK] = 1.0
    kfeat[:, ns] = key % SLC_BLOCK

    kv = lambda c: pl.BlockSpec((1, t, KV_LANES), lambda b, i: (b, 0, c))
    cst = lambda a: pl.BlockSpec(a.shape, lambda b, i: (0,) * a.ndim)
    consts = [jnp.asarray(slope_mat), jnp.asarray(dq_mat), jnp.asarray(slope_c),
              jnp.asarray(dc_mat), mmap, jnp.asarray(kfeat, BF16)]
    return pl.pallas_call(
        functools.partial(_nsa_prompt_body, tq=tq, ns=ns, n_sel=n_sel),
        grid=(bsz, t // tq),
        in_specs=[pl.BlockSpec((1, tq, NSA_Q), lambda b, i: (b, i, 0)),
                  pl.BlockSpec((1, 1, nch, KV_LANES), lambda b, i: (0, b, 0, 0)),
                  pl.BlockSpec((1, 1, nch, KV_LANES), lambda b, i: (1, b, 0, 0)),
                  kv(2), kv(3), kv(0), kv(1),
                  pl.BlockSpec((1, tq, LANES), lambda b, i: (b, i, 0))] + [cst(a) for a in consts],
        out_specs=pl.BlockSpec((1, tq, NSA_Q), lambda b, i: (b, i, 0)),
        out_shape=_sds((bsz, t, NSA_Q)),
        scratch_shapes=[pltpu.VMEM((NSA_KVH, m_rows, 2 * LANES), BF16),
                        pltpu.VMEM((NSA_KVH, m_rows, 2 * LANES), BF16),
                        pltpu.VMEM((2, NSA_KVH, m_rows, LANES), F32),
                        pltpu.VMEM((2, NSA_KVH, m_rows, KV_LANES), F32),
                        pltpu.VMEM((NSA_KVH, m_rows, KV_LANES), F32)],
        compiler_params=_params(("arbitrary", "arbitrary")),
        name="nsa_prompt",
    )(qn, kcvc, kcvc, rows, rows, winr, winr, gts, *consts)


def _nsa_sample_body(pt_ref, q_ref, kc_ref, vc_ref, rows_ref, winr_ref, swt_ref, g_ref,
                     slope_ref, pos_ref, cur_ref, dc_ref, gsum_ref, mmap_ref, expand_ref,
                     cache_ref, o_ref, wout_ref, kvbuf, sem, *, past, t, npages, ns, n_sel):
    b = pl.program_id(0)
    nb = pl.num_programs(0)
    page = past // npages
    m_rows = NSA_HEADS * t
    half = m_rows // NSA_KVH
    pw = swt_ref.shape[-1]

    def page_copy(bb, slot, p):
        return pltpu.make_async_copy(cache_ref.at[pt_ref[bb, p], pl.ds(2, 2)],
                                     kvbuf.at[slot, :, :, pl.ds(p * page, page)], sem.at[slot])

    @pl.when(b == 0)
    def _():
        for p in range(npages):
            page_copy(0, 0, p).start()

    @pl.when(b + 1 < nb)
    def _():
        for p in range(npages):
            page_copy(b + 1, (b + 1) % 2, p).start()

    pieces = []
    for hh in range(NSA_HEADS):
        qg = q_ref[0, :, hh * NSA_DH:(hh + 1) * NSA_DH]
        z = jnp.zeros_like(qg)
        pieces.append(jnp.concatenate([qg, z] if hh < NSA_G else [z, qg], axis=1))
    qbd = jnp.concatenate(pieces, axis=0).astype(BF16)
    slope = slope_ref[...]
    pos = pos_ref[...]

    def widen(x, n):
        return jnp.concatenate([x] * (n // LANES), axis=1)

    def softmax(parts):
        mx = None
        for s in parts:
            r = jnp.max(s, axis=-1, keepdims=True)
            mx = r if mx is None else jnp.maximum(mx, r)
        ps = [jnp.where(s > 0.5 * NEG, jnp.exp(s - mx), 0.0) for s in parts]
        tot = None
        for p in ps:
            r = jnp.sum(p, axis=-1, keepdims=True)
            tot = r if tot is None else tot + r
        inv = 1.0 / jnp.maximum(tot, 1e-30)
        return [p * inv for p in ps]

    d_c = dc_ref[...]
    nch = d_c.shape[1]
    s_c = _dot_nt(qbd, kc_ref[0, 0])
    (p_c,) = softmax([jnp.where(d_c >= 0.0, s_c - widen(slope, nch) * d_c, NEG)])
    o_c = _dot(p_c, vc_ref[0, 0])

    imp = _dot_exact_lhs(gsum_ref[...], _dot_exact_rhs(p_c, mmap_ref[...]))
    ns8 = imp.shape[1]
    imp = jnp.concatenate([imp, jnp.zeros((m_rows, 2 * LANES - ns8), F32)], axis=1)
    imp = jnp.concatenate([imp, jnp.zeros((LANES - m_rows, 2 * LANES), F32)], axis=0)
    imp_t = imp.T[0:ns8, 0:m_rows]
    blk = lax.broadcasted_iota(jnp.int32, (ns8, m_rows), 0)
    cur = cur_ref[...]
    forced = (blk == 0) | (blk == cur) | (blk == cur - 1)
    score = jnp.where(blk <= cur, jnp.where(forced, SEL_FORCE, imp_t), -SEL_FORCE)
    score = jnp.where(blk < ns, score, PAD_SCORE)
    sel_t = _select_top(score, ns, n_sel)
    sel_t = jnp.concatenate([sel_t, jnp.zeros((ns8, LANES - m_rows), F32)], axis=1)
    sel_t = jnp.concatenate([sel_t, jnp.zeros((2 * LANES - ns8, LANES), F32)], axis=0)
    sel = sel_t.T[0:m_rows, 0:ns8]

    new_ks = rows_ref[0, :, 2 * KV_LANES:3 * KV_LANES]
    new_vs = rows_ref[0, :, 3 * KV_LANES:4 * KV_LANES]
    new_kw = winr_ref[0, :, 0:KV_LANES]
    new_vw = winr_ref[0, :, KV_LANES:2 * KV_LANES]
    dist_n = pos[:, 0:t] - (past + lax.broadcasted_iota(jnp.int32, (m_rows, t), 1)).astype(F32)
    slope_n = slope[:, 0:t]

    kwt = swt_ref[0, 0]
    vwt = swt_ref[0, 1]
    dist_w = widen(pos, pw) - (lax.broadcasted_iota(jnp.int32, (m_rows, pw), 1)
                               + (past - pw)).astype(F32)
    s_w = _dot(qbd, kwt)
    s_w = jnp.where(dist_w >= 0.0,
                    jnp.where(dist_w <= float(WINDOW), s_w - widen(slope, pw) * dist_w, NEG), NEG)
    s_wn = _dot_nt(qbd, new_kw)
    s_wn = jnp.where(dist_n >= 0.0,
                     jnp.where(dist_n <= float(WINDOW), s_wn - slope_n * dist_n, NEG), NEG)
    p_w, p_wn = softmax([s_w, s_wn])
    o_w = _dot_nt(p_w, vwt) + _dot(p_wn, new_vw)
    w_rows = jnp.concatenate([jnp.concatenate([kwt.T, vwt.T], axis=1),
                              winr_ref[0]], axis=0)
    wout_ref[0] = w_rows[pw + t - wout_ref.shape[1]:, :]

    slot = b % 2
    for p in range(npages):
        page_copy(b, slot, p).wait()
    nfull = past // SLC_BLOCK
    mask = jnp.dot(sel.astype(BF16), expand_ref[...], preferred_element_type=F32)
    dist = widen(pos, past) - lax.broadcasted_iota(jnp.int32, (m_rows, past), 1).astype(F32)
    s_s = _dot(qbd, kvbuf[slot, 0])
    s_s = jnp.where(mask > 0.5, jnp.where(dist >= 0.0, s_s - widen(slope, past) * dist, NEG), NEG)
    s_sn = _dot_nt(qbd, new_ks)
    s_sn = jnp.where(sel[:, nfull:nfull + 1] > 0.5,
                     jnp.where(dist_n >= 0.0, s_sn - slope_n * dist_n, NEG), NEG)
    p_s, p_sn = softmax([s_s, s_sn])
    o_s = _dot_nt(p_s, kvbuf[slot, 1]) + _dot(p_sn, new_vs)

    def own(x):
        return jnp.concatenate([x[0:half, 0:NSA_DH], x[half:, NSA_DH:2 * NSA_DH]], axis=0)

    gates = g_ref[0]

    def gate_col(br):
        return jnp.concatenate(
            [gates[:, GATE_COL0 + hh * 3 + br:GATE_COL0 + hh * 3 + br + 1] for hh in range(NSA_HEADS)],
            axis=0)

    mix = gate_col(0) * own(o_c) + gate_col(1) * own(o_s) + gate_col(2) * own(o_w)
    o_ref[0] = jnp.concatenate([mix[hh * t:(hh + 1) * t, :] for hh in range(NSA_HEADS)], axis=1)


def _nsa_sample(qn, kcvc, rows, winr, state_win_t, gts, cache_t, page_table):
    bsz, t, _ = qn.shape
    npages = page_table.shape[1]
    page = cache_t.shape[-1]
    past = npages * page
    pw = state_win_t.shape[-1]
    nch = kcvc.shape[2]
    ns = -(-(past + t) // SLC_BLOCK)
    ns8 = -(-ns // 8) * 8
    m_rows = NSA_HEADS * t
    assert past % SLC_BLOCK == 0 and t <= SLC_BLOCK and t % 8 == 0 and pw + t >= WINDOW
    assert m_rows <= LANES and ns8 <= 2 * LANES and nch % LANES == 0 and pw % LANES == 0
    n_sel = min(SLC_TOPN, ns)
    slopes = _alibi_slopes().reshape(-1)
    lanes1 = np.ones((1, LANES), np.float32)
    slope_b = np.repeat(slopes, t)[:, None].astype(np.float32) * lanes1
    pos_i = np.tile(past + np.arange(t), NSA_HEADS)
    cmp_end = np.arange(nch, dtype=np.float32) * CMP_STRIDE + (CMP_LEN - 1)
    dc_mat = pos_i[:, None].astype(np.float32) - cmp_end[None, :]
    row_kvh = np.arange(m_rows) // (NSA_G * t)
    row_tok = np.arange(m_rows) % t
    gsum = ((row_kvh[:, None] == row_kvh[None, :]) & (row_tok[:, None] == row_tok[None, :]))
    mmap = np.zeros((nch, ns8), np.float32)
    mmap[:, :ns] = _cmp_to_slc_map(nch, ns)
    expand = (np.arange(ns8)[:, None] == (np.arange(past) // SLC_BLOCK)[None, :])
    consts = [jnp.asarray(slope_b), jnp.asarray(pos_i[:, None].astype(np.float32) * lanes1),
              jnp.asarray((pos_i // SLC_BLOCK)[None, :].astype(np.int32)), jnp.asarray(dc_mat),
              jnp.asarray(gsum.astype(np.float32), BF16), jnp.asarray(mmap, BF16),
              jnp.asarray(expand.astype(np.float32), BF16)]

    cst = lambda a: pl.BlockSpec(a.shape, lambda b, pt: (0,) * a.ndim)
    grid_spec = pltpu.PrefetchScalarGridSpec(
        num_scalar_prefetch=1,
        grid=(bsz,),
        in_specs=[pl.BlockSpec((1, t, NSA_Q), lambda b, pt: (b, 0, 0)),
                  pl.BlockSpec((1, 1, nch, KV_LANES), lambda b, pt: (0, b, 0, 0)),
                  pl.BlockSpec((1, 1, nch, KV_LANES), lambda b, pt: (1, b, 0, 0)),
                  pl.BlockSpec((1, t, ROWS_LANES), lambda b, pt: (b, 0, 0)),
                  pl.BlockSpec((1, t, 2 * KV_LANES), lambda b, pt: (b, 0, 0)),
                  pl.BlockSpec((1, 2, KV_LANES, pw), lambda b, pt: (b, 0, 0, 0)),
                  pl.BlockSpec((1, t, LANES), lambda b, pt: (b, 0, 0))]
        + [cst(a) for a in consts] + [pl.BlockSpec(memory_space=pl.ANY)],
        out_specs=[pl.BlockSpec((1, t, NSA_Q), lambda b, pt: (b, 0, 0)),
                   pl.BlockSpec((1, WINDOW, 2 * KV_LANES), lambda b, pt: (b, 0, 0))],
        scratch_shapes=[pltpu.VMEM((2, 2, KV_LANES, past), F32), pltpu.SemaphoreType.DMA((2,))],
    )
    return pl.pallas_call(
        functools.partial(_nsa_sample_body, past=past, t=t, npages=npages, ns=ns, n_sel=n_sel),
        grid_spec=grid_spec,
        out_shape=[_sds((bsz, t, NSA_Q)), _sds((bsz, WINDOW, 2 * KV_LANES))],
        compiler_params=_params(("arbitrary",)),
        name="nsa_sample",
    )(page_table, qn, kcvc, kcvc, rows, winr, state_win_t, gts, *consts, cache_t)


def _ffn_body(x_ref, og_ref, on_ref, gtm_ref, scf_ref, shf_ref, gtf_ref,
              nmp_ref, nfp_ref, nfo_ref, wo_ref, wu_ref, wd_ref, y_ref, *, ff_chunk):
    mix = (jnp.dot(og_ref[0].astype(BF16), wo_ref[0:GLA_V, :], preferred_element_type=F32)
           + jnp.dot(on_ref[0].astype(BF16), wo_ref[GLA_V:GLA_V + NSA_Q, :],
                     preferred_element_type=F32))
    x1 = x_ref[0] + gtm_ref[0] * _rms(mix, nmp_ref[...])
    hb = (_rms(x1, nfp_ref[...]) * (1.0 + scf_ref[0]) + shf_ref[0]).astype(BF16)
    f = None
    for c in range(wu_ref.shape[1] // ff_chunk):
        cs = slice(c * ff_chunk, (c + 1) * ff_chunk)
        u = jnp.maximum(jnp.dot(hb, wu_ref[:, cs], preferred_element_type=F32), 0.0)
        term = jnp.dot((u * u).astype(BF16), wd_ref[cs, :], preferred_element_type=F32)
        f = term if f is None else f + term
    y_ref[0] = x1 + gtf_ref[0] * _rms(f, nfo_ref[...])


def _ffn(x, og, on, gtm, scf, shf, gtf, nmp, nfp, nfo, wo, wu, wd):
    bsz, t, d = x.shape
    tq = min(t, 512)
    assert t % tq == 0
    tok = lambda n: pl.BlockSpec((1, tq, n), lambda b, i: (b, i, 0))
    full = lambda a: pl.BlockSpec(a.shape, lambda b, i: (0,) * a.ndim,
                                  pipeline_mode=pl.Buffered(1))
    vec = lambda a: pl.BlockSpec(a.shape, lambda b, i: (0,) * a.ndim)
    return pl.pallas_call(
        functools.partial(_ffn_body, ff_chunk=min(wu.shape[1], 1024)),
        grid=(bsz, t // tq),
        in_specs=[tok(d), tok(GLA_V), tok(NSA_Q), _mod_spec(gtm, tq), _mod_spec(scf, tq),
                  _mod_spec(shf, tq), _mod_spec(gtf, tq), vec(nmp), vec(nfp), vec(nfo),
                  full(wo), full(wu), full(wd)],
        out_specs=tok(d),
        out_shape=_sds((bsz, t, d)),
        compiler_params=_params(("arbitrary", "arbitrary")),
        name="ffn",
    )(x, og, on, gtm, scf, shf, gtf, nmp, nfp, nfo, wo, wu, wd)


def _permute_w_in(w_in):
    o_zq = 0
    o_zk = o_zq + GLA_QK
    o_zv = o_zk + GLA_QK
    o_za = o_zv + GLA_V
    o_zr = o_za + GLA_RANK
    o_zqn = o_zr + GLA_V
    o_zkv = o_zqn + NSA_Q
    o_zg = o_zkv + NSA_KV
    o_end = o_zg + NSA_GATE
    d = w_in.shape[0]
    pad = jnp.zeros((d, LANES - GLA_RANK - NSA_GATE), w_in.dtype)
    cols = [w_in[:, o_zq:o_za], w_in[:, o_zr:o_zg], w_in[:, o_za:o_zr], w_in[:, o_zg:o_end], pad]
    return jnp.concatenate(cols, axis=1).astype(BF16)


def _layer_weights(l, norm_mix_pre, norm_mix_post, norm_ffn_pre, norm_ffn_post, w_in,
                   gla_w_gate, gla_b_gate, gla_norm, cmp_pos, cmp_w1, cmp_b1, cmp_w2, cmp_b2,
                   w_out, w_up, w_down):
    wg_pad = jnp.zeros((LANES, GLA_QK), F32).at[:GLA_RANK].set(gla_w_gate[l]).astype(BF16)
    w1, w2 = _compress_weights(cmp_w1[l], cmp_w2[l])
    return dict(
        nmpre=norm_mix_pre[l][None, :], nmpost=norm_mix_post[l][None, :],
        nfpre=norm_ffn_pre[l][None, :], nfpost=norm_ffn_post[l][None, :],
        w_perm=_permute_w_in(w_in[l]), wg_pad=wg_pad, bg=gla_b_gate[l][None, :],
        gnorm=gla_norm[l][None, :], w1=w1, w2=w2, posb=_posbias(cmp_pos[l], cmp_w1[l]),
        b1=cmp_b1[l][:, None, :], b2=cmp_b2[l][:, None, :],
        wo=w_out[l].astype(BF16), wu=w_up[l].astype(BF16), wd=w_down[l].astype(BF16))


def _split_ada(ada, rows_per_batch):
    parts = jnp.split(ada, 6, axis=-1)
    if rows_per_batch is None:
        return [p[:, None, :] for p in parts]
    return [jnp.repeat(p, rows_per_batch, axis=0)[None] for p in parts]


def _prompt_layer(x, ada, w, wbuf):
    bsz, t, d = x.shape
    sh_m, sc_m, gt_m, sh_f, sc_f, gt_f = _split_ada(ada, None)
    qg, kg, vg, rg, qn, rows, winr, logf, gts = _inproj(x, sc_m, sh_m, w["nmpre"], w["w_perm"],
                                                        w["wg_pad"], w["bg"])
    s0 = jnp.zeros((bsz, GLA_HEADS, GLA_DK, GLA_DV), F32)
    og, s_fin = _gla(qg, kg, vg, logf, rg, s0, w["gnorm"])
    kcvc = _compress_prompt(rows, w["w1"], w["posb"], w["b1"], w["w2"], w["b2"])
    on = _nsa_prompt(qn, kcvc, rows, winr, gts)
    y = _ffn(x, og, on, gt_m, sc_f, sh_f, gt_f, w["nmpost"], w["nfpre"], w["nfpost"],
             w["wo"], w["wu"], w["wd"])
    win = winr.reshape(bsz, t, 2, NSA_KVH, NSA_DH)
    if t < wbuf:
        win = jnp.pad(win, ((0, 0), (wbuf - t, 0), (0, 0), (0, 0), (0, 0)))
    return y, rows.reshape(bsz, t, 4, NSA_KVH, NSA_DH), win[:, -wbuf:], s_fin


def _sample_layer(x, ada, cache, page_table, state_win, state_gla, w):
    bsz, t, d = x.shape
    n = bsz * t
    sh_m, sc_m, gt_m, sh_f, sc_f, gt_f = _split_ada(ada, t)
    outs = _inproj(x.reshape(1, n, d), sc_m, sh_m, w["nmpre"], w["w_perm"], w["wg_pad"], w["bg"])
    qg, kg, vg, rg, qn, rows, winr, logf, gts = [o.reshape(bsz, t, o.shape[-1]) for o in outs]
    og, s_fin = _gla(qg, kg, vg, logf, rg, state_gla, w["gnorm"])
    n_pool, page = cache.shape[0], cache.shape[1]
    cache_t = jnp.transpose(cache, (0, 2, 3, 4, 1)).reshape(n_pool, 4, KV_LANES, page)
    pw = state_win.shape[1]
    state_win_t = jnp.transpose(state_win, (0, 2, 3, 4, 1)).reshape(bsz, 2, KV_LANES, pw)
    kcvc = _compress_paged(cache_t, page_table, w["w1"], w["posb"], w["b1"], w["w2"], w["b2"])
    on, win = _nsa_sample(qn, kcvc, rows, winr, state_win_t, gts, cache_t, page_table)
    y = _ffn(x.reshape(1, n, d), og.reshape(1, n, GLA_V), on.reshape(1, n, NSA_Q),
             gt_m, sc_f, sh_f, gt_f, w["nmpost"], w["nfpre"], w["nfpost"],
             w["wo"], w["wu"], w["wd"])
    return (y.reshape(bsz, t, d), rows.reshape(bsz, t, 4, NSA_KVH, NSA_DH),
            win.reshape(bsz, WINDOW, 2, NSA_KVH, NSA_DH), s_fin)


def kernel(x_prompt, x_sample, cache_kv, state_win, state_gla, page_table, c_prompt, c_sample,
           norm_mix_pre, norm_mix_post, norm_ffn_pre, norm_ffn_post, w_ada, b_ada, w_in,
           gla_w_gate, gla_b_gate, gla_norm, cmp_pos, cmp_w1, cmp_b1, cmp_w2, cmp_b2,
           w_out, w_up, w_down):
    depth = w_in.shape[0]
    bsz = x_prompt.shape[0]
    wbuf = state_win.shape[2]
    assert wbuf == WINDOW
    c_all = jnp.concatenate([c_prompt, c_sample], axis=0)
    y_p, y_s = x_prompt, x_sample
    outs = [[] for _ in range(6)]
    for l in range(depth):
        w = _layer_weights(l, norm_mix_pre, norm_mix_post, norm_ffn_pre, norm_ffn_post, w_in,
                           gla_w_gate, gla_b_gate, gla_norm, cmp_pos, cmp_w1, cmp_b1, cmp_w2,
                           cmp_b2, w_out, w_up, w_down)
        ada = _ada(c_all, w_ada[l], b_ada[l])
        y_p, r_p, w_p, s_p = _prompt_layer(y_p, ada[:bsz], w, wbuf)
        y_s, r_s, w_s, s_s = _sample_layer(y_s, ada[bsz:], cache_kv[l], page_table,
                                           state_win[l], state_gla[l], w)
        for lst, val in zip(outs, (r_p, r_s, w_p, w_s, s_p, s_s)):
            lst.append(val)
    return (y_p, y_s) + tuple(jnp.stack(o) for o in outs)
```

```python
import functools
import math

import numpy as np
import jax
import jax.numpy as jnp
from jax import lax
from jax.experimental import pallas as pl
from jax.experimental.pallas import tpu as pltpu

F32 = jnp.float32
BF16 = jnp.bfloat16

GLA_HEADS = 4
GLA_DK = 64
GLA_DV = 128
GLA_RANK = 16
GLA_NORMALIZER = 16.0
GLA_CHUNK = 64
NSA_HEADS = 8
NSA_DH = 64
NSA_KVH = 2
NSA_G = NSA_HEADS // NSA_KVH
CMP_STRIDE = 16
CMP_LEN = 32
CMP_HIDDEN = 256
SLC_BLOCK = 64
SLC_TOPN = 16
SEL_FORCE = 1.0e4
WINDOW = 512
EPS = 1e-6
NEG = -1.0e30
PAD_SCORE = -3.0e4
MASK_BIAS = 2.0 ** 60
LANES = 128
VMEM_LIMIT = 56 * 1024 * 1024

GLA_QK = GLA_HEADS * GLA_DK
GLA_V = GLA_HEADS * GLA_DV
NSA_Q = NSA_HEADS * NSA_DH
NSA_KV = 6 * NSA_KVH * NSA_DH
NSA_GATE = 3 * NSA_HEADS
KV_LANES = NSA_KVH * NSA_DH
ROWS_LANES = 4 * KV_LANES


def _sds(shape, dtype=F32):
    return jax.ShapeDtypeStruct(shape, dtype)


def _params(sem):
    return pltpu.CompilerParams(dimension_semantics=sem, vmem_limit_bytes=VMEM_LIMIT)


def _dot(a, b):
    return jnp.dot(a.astype(BF16), b.astype(BF16), preferred_element_type=F32)


def _dot_nt(a, b):
    return lax.dot_general(a.astype(BF16), b.astype(BF16), (((1,), (1,)), ((), ())),
                           preferred_element_type=F32)


def _dot_tn(a, b):
    return lax.dot_general(a.astype(BF16), b.astype(BF16), (((0,), (0,)), ((), ())),
                           preferred_element_type=F32)


def _split3(x):
    p1 = x.astype(BF16)
    r1 = x - p1.astype(F32)
    p2 = r1.astype(BF16)
    p3 = (r1 - p2.astype(F32)).astype(BF16)
    return p1, p2, p3


def _dot_f32(a, b):
    a1, a2, _ = _split3(a)
    b1, b2, _ = _split3(b)
    return (jnp.dot(a1, b1, preferred_element_type=F32)
            + jnp.dot(a1, b2, preferred_element_type=F32)
            + jnp.dot(a2, b1, preferred_element_type=F32))


def _dot_exact_lhs(a_bf16, x):
    out = None
    for p in _split3(x):
        t = jnp.dot(a_bf16, p, preferred_element_type=F32)
        out = t if out is None else out + t
    return out


def _dot_exact_rhs(x, b_bf16):
    out = None
    for p in _split3(x):
        t = jnp.dot(p, b_bf16, preferred_element_type=F32)
        out = t if out is None else out + t
    return out


def _sigmoid(x):
    return 1.0 / (1.0 + jnp.exp(-x))


def _silu(x):
    return x * _sigmoid(x)


def _rms(x, g):
    return x * lax.rsqrt(jnp.mean(x * x, axis=-1, keepdims=True) + EPS) * g


def _ada_body(c_ref, w_ref, b_ref, o_ref):
    o_ref[...] = _dot_f32(_silu(c_ref[...]), w_ref[...]) + b_ref[...]


def _ada(c, w_ada, b_ada):
    n, d = c.shape
    nout = w_ada.shape[1]
    tn = d
    return pl.pallas_call(
        _ada_body,
        grid=(nout // tn,),
        in_specs=[pl.BlockSpec((n, d), lambda j: (0, 0)),
                  pl.BlockSpec((d, tn), lambda j: (0, j)),
                  pl.BlockSpec((1, tn), lambda j: (0, j))],
        out_specs=pl.BlockSpec((n, tn), lambda j: (0, j)),
        out_shape=_sds((n, nout)),
        compiler_params=_params(("arbitrary",)),
        name="ada",
    )(c, w_ada, b_ada.reshape(1, nout))


_C_ZQ = 0
_C_ZK = _C_ZQ + GLA_QK
_C_ZV = _C_ZK + GLA_QK
_C_ZR = _C_ZV + GLA_V
_C_ZQN = _C_ZR + GLA_V
_C_ZKV = _C_ZQN + NSA_Q
_C_MISC = _C_ZKV + NSA_KV
_C_END = _C_MISC + LANES
GATE_COL0 = GLA_RANK


def _inproj_body(x_ref, sc_ref, sh_ref, g_ref, w_ref, wg_ref, bg_ref,
                 qg_ref, kg_ref, vg_ref, rg_ref, qn_ref, rows_ref, winr_ref, logf_ref, gts_ref):
    h = _rms(x_ref[0], g_ref[...]) * (1.0 + sc_ref[0]) + sh_ref[0]
    hb = h.astype(BF16)

    def proj(a, b):
        return jnp.dot(hb, w_ref[:, a:b], preferred_element_type=F32)

    qg_ref[0] = proj(_C_ZQ, _C_ZK) * (GLA_DK ** -0.5)
    kg_ref[0] = proj(_C_ZK, _C_ZV)
    vg_ref[0] = proj(_C_ZV, _C_ZR)
    rg_ref[0] = _silu(proj(_C_ZR, _C_ZQN))
    qn_ref[0] = proj(_C_ZQN, _C_ZKV) * (NSA_DH ** -0.5)
    rows_ref[0] = proj(_C_ZKV, _C_ZKV + ROWS_LANES)
    winr_ref[0] = proj(_C_ZKV + ROWS_LANES, _C_MISC)
    zm = proj(_C_MISC, _C_END)
    gts_ref[0] = _sigmoid(zm)
    logit = jnp.dot(zm.astype(BF16), wg_ref[...], preferred_element_type=F32) + bg_ref[...]
    log_sig = jnp.minimum(logit, 0.0) - jnp.log(1.0 + jnp.exp(-jnp.abs(logit)))
    logf_ref[0] = log_sig * (1.0 / GLA_NORMALIZER)


def _mod_spec(mod, tq):
    d = mod.shape[-1]
    if mod.shape[1] == 1:
        return pl.BlockSpec((1, 1, d), lambda b, t: (b, 0, 0))
    return pl.BlockSpec((1, tq, d), lambda b, t: (b, t, 0))


def _inproj(x, sc, sh, gain, w_perm, wg_pad, bg):
    bsz, t, d = x.shape
    tq = min(t, 512)
    assert t % tq == 0
    widths = (GLA_QK, GLA_QK, GLA_V, GLA_V, NSA_Q, ROWS_LANES, NSA_KV - ROWS_LANES, GLA_QK, LANES)
    tok = lambda n: pl.BlockSpec((1, tq, n), lambda b, i: (b, i, 0))
    full = lambda a: pl.BlockSpec(a.shape, lambda b, i: (0,) * a.ndim)
    return pl.pallas_call(
        _inproj_body,
        grid=(bsz, t // tq),
        in_specs=[tok(d), _mod_spec(sc, tq), _mod_spec(sh, tq), full(gain), full(w_perm),
                  full(wg_pad), full(bg)],
        out_specs=[tok(n) for n in widths],
        out_shape=[_sds((bsz, t, n)) for n in widths],
        compiler_params=_params(("arbitrary", "arbitrary")),
        name="inproj",
    )(x, sc, sh, gain, w_perm, wg_pad, bg)


def _head_stack(x, head_of_lane):
    return jnp.concatenate([jnp.where(head_of_lane == h, x, 0.0) for h in range(GLA_HEADS)],
                           axis=0)


def _gla_body(q_ref, k_ref, v_ref, f_ref, r_ref, s0_ref, gn_ref, o_ref, sfin_ref, s_scr,
              *, rows, nchunks, nb):
    c_len = GLA_CHUNK
    t = pl.program_id(1)

    @pl.when(t == 0)
    def _():
        s_scr[...] = s0_ref[...]

    n = nchunks * c_len

    def load(ref, bb):
        x = ref[bb]
        if rows < n:
            x = jnp.concatenate([x, jnp.zeros((n - rows, x.shape[1]), F32)], axis=0)
        return x

    ri = lax.broadcasted_iota(jnp.int32, (n, n), 0)
    ci = lax.broadcasted_iota(jnp.int32, (n, n), 1)
    shift = c_len.bit_length() - 1
    same_chunk = lax.shift_right_logical(ri, shift) == lax.shift_right_logical(ci, shift)
    tril = jnp.where(ci <= ri, jnp.where(same_chunk, 1.0, 0.0), 0.0).astype(BF16)
    causal = (lax.broadcasted_iota(jnp.int32, (c_len, c_len), 1)
              <= lax.broadcasted_iota(jnp.int32, (c_len, c_len), 0))
    head_of_lane = lax.broadcasted_iota(jnp.int32, (c_len, GLA_QK), 1) // GLA_DK

    seqs = []
    for bb in range(nb):
        q, k, v, g, r = (load(ref, bb) for ref in (q_ref, k_ref, v_ref, f_ref, r_ref))
        b = None
        for gp in _split3(g):
            term = jnp.dot(tril, gp, preferred_element_type=F32)
            b = term if b is None else b + term
        seqs.append((q, k, v, r, b, [s_scr[bb, h] for h in range(GLA_HEADS)]))

    for c in range(nchunks):
        cs = slice(c * c_len, (c + 1) * c_len)
        lo = c * c_len
        for bb, (q, k, v, r, b, state) in enumerate(seqs):
            b_c = b[cs]
            b_last = b_c[c_len - 1:c_len, :]
            decay_t = jnp.exp(jnp.broadcast_to(b_last, (GLA_DV, GLA_QK)).T)
            b_mid = b_c[c_len // 2 - 1:c_len // 2, :]
            q_inter = q[cs] * jnp.exp(b_c)
            q_intra = q[cs] * jnp.exp(b_c - b_mid)
            k_intra = k[cs] * jnp.exp(b_mid - b_c)
            k_out = k[cs] * jnp.exp(b_last - b_c)
            v_c = v[cs]
            k_out_t = k_out.T
            a_all = _dot_nt(_head_stack(q_intra, head_of_lane), k_intra)
            o_all = _dot(_head_stack(q_inter, head_of_lane), jnp.concatenate(state, axis=0))
            outs = []
            for h in range(GLA_HEADS):
                ks = slice(h * GLA_DK, (h + 1) * GLA_DK)
                vs = slice(h * GLA_DV, (h + 1) * GLA_DV)
                hs = slice(h * c_len, (h + 1) * c_len)
                a = jnp.where(causal, a_all[hs], 0.0)
                av = _dot(jnp.concatenate([a, k_out_t[ks, :]], axis=0), v_c[:, vs])
                o = av[0:c_len] + o_all[hs]
                state[h] = state[h] * decay_t[ks, :] + av[c_len:]
                outs.append(_rms(o, gn_ref[...]))
            on = jnp.concatenate(outs, axis=1) * r[cs]
            o_ref[bb, lo:min(lo + c_len, rows), :] = on[:min(c_len, rows - lo)]
    for bb, seq in enumerate(seqs):
        for h in range(GLA_HEADS):
            s_scr[bb, h] = seq[5][h]

    @pl.when(t == pl.num_programs(1) - 1)
    def _():
        sfin_ref[...] = s_scr[...]


def _gla(qg, kg, vg, logf, rg, s0, gnorm):
    bsz, t, _ = qg.shape
    tt = min(t, 4 * GLA_CHUNK)
    assert t % tt == 0 and (tt % GLA_CHUNK == 0 or tt < GLA_CHUNK)
    nchunks = max(tt // GLA_CHUNK, 1)
    nb = math.gcd(bsz, 4 if nchunks > 1 else 16)
    tok = lambda n: pl.BlockSpec((nb, tt, n), lambda b, i: (b, i, 0))
    st = pl.BlockSpec((nb, GLA_HEADS, GLA_DK, GLA_DV), lambda b, i: (b, 0, 0, 0))
    return pl.pallas_call(
        functools.partial(_gla_body, rows=tt, nchunks=nchunks, nb=nb),
        grid=(bsz // nb, t // tt),
        in_specs=[tok(GLA_QK), tok(GLA_QK), tok(GLA_V), tok(GLA_QK), tok(GLA_V), st,
                  pl.BlockSpec((1, GLA_DV), lambda b, i: (0, 0))],
        out_specs=[tok(GLA_V), st],
        out_shape=[_sds((bsz, t, GLA_V)), _sds((bsz, GLA_HEADS, GLA_DK, GLA_DV))],
        scratch_shapes=[pltpu.VMEM((nb, GLA_HEADS, GLA_DK, GLA_DV), F32)],
        compiler_params=_params(("arbitrary", "arbitrary")),
        name="gla",
    )(qg, kg, vg, logf, rg, s0, gnorm)


def _posbias_body(p_ref, w_ref, o_ref):
    o_ref[0, 0] = _dot_f32(p_ref[0, 0], w_ref[0, 0])


def _posbias(cmp_pos, cmp_w1):
    kdim = CMP_STRIDE * NSA_DH
    pos = jnp.broadcast_to(cmp_pos.reshape(2, 2, 1, kdim), (2, 2, 8, kdim))
    w = cmp_w1.reshape(2, 2, kdim, CMP_HIDDEN)
    return pl.pallas_call(
        _posbias_body,
        grid=(2, 2),
        in_specs=[pl.BlockSpec((1, 1, 8, kdim), lambda a, b: (a, b, 0, 0)),
                  pl.BlockSpec((1, 1, kdim, CMP_HIDDEN), lambda a, b: (a, b, 0, 0))],
        out_specs=pl.BlockSpec((1, 1, 8, CMP_HIDDEN), lambda a, b: (a, b, 0, 0)),
        out_shape=_sds((2, 2, 8, CMP_HIDDEN)),
        compiler_params=_params(("arbitrary", "arbitrary")),
        name="posbias",
    )(pos, w)


def _compress_rows(x_ref, w1_ref, pb_ref, b1_ref, w2_ref, b2_ref):
    nch = x_ref.shape[0] // CMP_STRIDE
    lo = lax.broadcasted_iota(jnp.int32, (nch, KV_LANES), 1) < NSA_DH
    cols = [[], []]
    for jp in range(CMP_STRIDE // 2):
        xe = x_ref[pl.ds(2 * jp, nch, stride=CMP_STRIDE), :]
        xo = x_ref[pl.ds(2 * jp + 1, nch, stride=CMP_STRIDE), :]
        cols[0].append(jnp.where(lo, xe, pltpu.roll(xo, NSA_DH, axis=1)).astype(BF16))
        cols[1].append(jnp.where(lo, pltpu.roll(xe, NSA_DH, axis=1), xo).astype(BF16))
    bias = pb_ref[0, 0, 0:1, :] + pb_ref[0, 1, 0:1, :] + b1_ref[0]
    outs = []
    for h in range(NSA_KVH):
        z = jnp.dot(jnp.concatenate(cols[h], axis=1), w1_ref[0], preferred_element_type=F32)
        z_second = z[:, CMP_HIDDEN:2 * CMP_HIDDEN]
        z_next = jnp.concatenate([z_second[1:], jnp.zeros((1, CMP_HIDDEN), F32)], axis=0)
        pre = z[:, 0:CMP_HIDDEN] + z_next + bias
        outs.append(_dot(_silu(pre), w2_ref[0]) + b2_ref[0])
    return jnp.concatenate(outs, axis=1)


def _compress_weights(cmp_w1, cmp_w2):
    kdim = CMP_STRIDE * NSA_DH
    w1 = jnp.concatenate([cmp_w1[:, :CMP_STRIDE].reshape(2, kdim, CMP_HIDDEN),
                          cmp_w1[:, CMP_STRIDE:].reshape(2, kdim, CMP_HIDDEN)], axis=-1)
    return w1.astype(BF16), cmp_w2.astype(BF16)


def _compress_wspec(shape):
    nd = len(shape)
    return pl.BlockSpec((1,) + shape[1:], lambda b, cc, *_: (cc,) + (0,) * (nd - 1))


def _compress_prompt_body(x_ref, w1_ref, pb_ref, b1_ref, w2_ref, b2_ref, o_ref):
    o_ref[0, 0] = _compress_rows(x_ref.at[0], w1_ref, pb_ref, b1_ref, w2_ref, b2_ref)


def _compress_prompt(rows, w1, posb, b1, w2, b2):
    bsz, t, _ = rows.shape
    assert t % LANES == 0
    nch = t // CMP_STRIDE
    weights = (w1, posb, b1, w2, b2)
    return pl.pallas_call(
        _compress_prompt_body,
        grid=(bsz, 2),
        in_specs=[pl.BlockSpec((1, t, KV_LANES), lambda b, cc: (b, 0, cc))]
        + [_compress_wspec(a.shape) for a in weights],
        out_specs=pl.BlockSpec((1, 1, nch, KV_LANES), lambda b, cc: (cc, b, 0, 0)),
        out_shape=_sds((2, bsz, nch, KV_LANES)),
        compiler_params=_params(("arbitrary", "arbitrary")),
        name="compress_prompt",
    )(rows, *weights)


def _compress_paged_body(pt_ref, cache_ref, w1_ref, pb_ref, b1_ref, w2_ref, b2_ref, o_ref,
                         xt_buf, x_scr, sem, *, npages):
    b = pl.program_id(0)
    cc = pl.program_id(1)
    step = b * 2 + cc
    nsteps = pl.num_programs(0) * 2
    page = xt_buf.shape[-1]

    def page_copy(bb, c, slot, p):
        return pltpu.make_async_copy(cache_ref.at[pt_ref[bb, p], c], xt_buf.at[slot, p],
                                     sem.at[slot])

    @pl.when(step == 0)
    def _():
        for p in range(npages):
            page_copy(0, 0, 0, p).start()

    @pl.when(step + 1 < nsteps)
    def _():
        nxt = step + 1
        for p in range(npages):
            page_copy(nxt // 2, nxt % 2, nxt % 2, p).start()

    slot = step % 2
    for p in range(npages):
        page_copy(b, cc, slot, p).wait()
    for p in range(npages):
        x_scr[p * page:(p + 1) * page, :] = xt_buf[slot, p].T
    o_ref[0, 0] = _compress_rows(x_scr, w1_ref, pb_ref, b1_ref, w2_ref, b2_ref)


def _compress_paged(cache_t, page_table, w1, posb, b1, w2, b2):
    bsz, npages = page_table.shape
    page = cache_t.shape[-1]
    assert page == LANES and cache_t.shape[2] == KV_LANES
    nch = npages * page // CMP_STRIDE
    weights = (w1, posb, b1, w2, b2)
    grid_spec = pltpu.PrefetchScalarGridSpec(
        num_scalar_prefetch=1,
        grid=(bsz, 2),
        in_specs=[pl.BlockSpec(memory_space=pl.ANY)] + [_compress_wspec(a.shape) for a in weights],
        out_specs=pl.BlockSpec((1, 1, nch, KV_LANES), lambda b, cc, pt: (cc, b, 0, 0)),
        scratch_shapes=[pltpu.VMEM((2, npages, KV_LANES, page), F32),
                        pltpu.VMEM((npages * page, KV_LANES), F32),
                        pltpu.SemaphoreType.DMA((2,))],
    )
    return pl.pallas_call(
        functools.partial(_compress_paged_body, npages=npages),
        grid_spec=grid_spec,
        out_shape=_sds((2, bsz, nch, KV_LANES)),
        compiler_params=_params(("arbitrary", "arbitrary")),
        name="compress_paged",
    )(page_table, cache_t, *weights)


def _cmp_to_slc_map(nc, ns):
    start = np.arange(nc) * CMP_STRIDE
    bs = np.arange(ns) * SLC_BLOCK
    ov = (np.minimum(start[:, None] + CMP_LEN, bs[None, :] + SLC_BLOCK)
          - np.maximum(start[:, None], bs[None, :]))
    return (np.clip(ov, 0, None) / CMP_LEN).astype(np.float32)


def _select_top(score_t, ns, n_sel):
    nrow, nq = score_t.shape
    groups = [score_t[8 * v:8 * v + 8, :] for v in range(nrow // 8)]
    ranks = [jnp.zeros((8, nq), F32) for _ in groups]
    sub = lax.broadcasted_iota(jnp.int32, (8, nq), 0)
    for i in range(ns):
        row = jnp.broadcast_to(score_t[i:i + 1, :], (8, nq))
        for v, grp in enumerate(groups):
            if i < 8 * v:
                ranks[v] = jnp.where(row >= grp, ranks[v] + 1.0, ranks[v])
            elif i >= 8 * v + 8:
                ranks[v] = jnp.where(row > grp, ranks[v] + 1.0, ranks[v])
            else:
                later = jnp.where(row >= grp, 1.0, 0.0)
                earlier = jnp.where(row > grp, 1.0, 0.0)
                ranks[v] = ranks[v] + jnp.where(sub > (i - 8 * v), later, earlier)
    rank = jnp.concatenate(ranks, axis=0)
    return jnp.where(rank < float(n_sel), 1.0, 0.0)


def _nsa_prompt_body(q_ref, kc_ref, vc_ref, ks_ref, vs_ref, kw_ref, vw_ref, g_ref,
                     slope_ref, dq_ref, cfeat_ref, dc_ref, mmap_ref, kfeat_ref,
                     o_ref, qs_scr, qw_scr, m_scr, acc_scr, oc_scr, *, tq, ns, n_sel):
    i = pl.program_id(1)
    t0 = i * tq
    tk = tq
    m_rows = NSA_G * tq
    nwin = WINDOW // tk
    lane_q = lax.broadcasted_iota(jnp.int32, (tq, KV_LANES), 1)
    lane_m = lax.broadcasted_iota(jnp.int32, (m_rows, LANES), 1)
    lane_v = lax.broadcasted_iota(jnp.int32, (tk, KV_LANES), 1)
    dq = dq_ref[...]
    rel_blk = ((lane_m - t0 // SLC_BLOCK) * SLC_BLOCK).astype(F32)
    SEL, WIN = 0, 1

    def tile(br, h, j, k_ref, v_ref, q_scr, mask, pen):
        k0 = pl.multiple_of(j * tk, tk)
        kt = jnp.concatenate([k_ref[0, pl.ds(k0, tk), :].astype(BF16),
                              kfeat_ref[pl.ds(k0, tk), :]], axis=1)
        s = lax.dot_general(q_scr[h], kt, (((1,), (1,)), ((), ())), preferred_element_type=F32)
        if mask == "causal":
            s = jnp.where(dq >= 0.0, s, NEG)
        elif mask == "far":
            s = jnp.where(dq <= 0.0, s, NEG)
        if pen is not None:
            s = s + pen
        vt = jnp.where((lane_v // NSA_DH) == h, v_ref[0, pl.ds(k0, tk), :], 1.0).astype(BF16)
        if mask == "causal":
            m_new = jnp.broadcast_to(jnp.max(s, axis=-1, keepdims=True), (m_rows, LANES))
            p = jnp.exp(s - jnp.concatenate([m_new] * (tk // LANES), axis=1))
            acc_scr[br, h] = jnp.dot(p.astype(BF16), vt, preferred_element_type=F32)
        else:
            m_old = m_scr[br, h]
            m_new = jnp.maximum(m_old, jnp.max(s, axis=-1, keepdims=True))
            p = jnp.exp(s - jnp.concatenate([m_new] * (tk // LANES), axis=1))
            acc_scr[br, h] = (jnp.exp(m_old - m_new) * acc_scr[br, h]
                              + jnp.dot(p.astype(BF16), vt, preferred_element_type=F32))
        m_scr[br, h] = m_new

    def finish(br, h):
        acc = acc_scr[br, h]
        return acc * (1.0 / pltpu.roll(acc, NSA_DH, axis=1))

    qbs = []
    for h in range(NSA_KVH):
        qs = []
        for g in range(NSA_G):
            c0 = (h * NSA_G + g) * NSA_DH
            qg = q_ref[0, :, c0:c0 + NSA_DH]
            qq = jnp.concatenate([qg, qg], axis=1)
            qs.append(jnp.where((lane_q // NSA_DH) == h, qq, 0.0))
        qbs.append(jnp.concatenate(qs, axis=0).astype(BF16))
        slope = slope_ref[h]
        feat_w = jnp.where(lane_m < ns, slope * rel_blk, jnp.where(lane_m == ns, slope, 0.0))
        qw_scr[h] = jnp.concatenate([qbs[h], feat_w.astype(BF16)], axis=1)

    for d in range(nwin + 1):
        pen = None if d == 0 else jnp.where(i >= d, 0.0, NEG)
        mask = "causal" if d == 0 else ("far" if d == nwin else None)
        for h in range(NSA_KVH):
            tile(WIN, h, jnp.maximum(i - d, 0), kw_ref, vw_ref, qw_scr, mask, pen)

    lowest = None
    for h in range(NSA_KVH):
        qb = qbs[h]

        d_c = dc_ref[...] + t0.astype(F32)
        feat_c = jnp.where(lane_m == 0, slope_ref[h] * float(CMP_STRIDE), 0.0).astype(BF16)
        kc_aug = jnp.concatenate([kc_ref[0, 0].astype(BF16), cfeat_ref[...]], axis=1)
        s_c = lax.dot_general(jnp.concatenate([qb, feat_c], axis=1), kc_aug,
                              (((1,), (1,)), ((), ())), preferred_element_type=F32)
        s_c = jnp.where(d_c >= 0.0, s_c, NEG)
        mx = jnp.max(s_c, axis=-1, keepdims=True)
        p = jnp.where(s_c > 0.5 * NEG, jnp.exp(s_c - mx), 0.0)
        p_c = p * (1.0 / jnp.maximum(jnp.sum(p, axis=-1, keepdims=True), 1e-30))
        oc_scr[h] = _dot(p_c, vc_ref[0, 0])

        p_sum = p_c[0:tq]
        for g in range(1, NSA_G):
            p_sum = p_sum + p_c[g * tq:(g + 1) * tq]
        imp = _dot_exact_rhs(p_sum, mmap_ref[...])
        blk = lax.broadcasted_iota(jnp.int32, (tq, ns), 1)
        cur = (t0 + lax.broadcasted_iota(jnp.int32, (tq, ns), 0)) // SLC_BLOCK
        forced = (blk == 0) | (blk == cur) | (blk == cur - 1)
        score = jnp.where(blk <= cur, jnp.where(forced, SEL_FORCE, imp), -SEL_FORCE)
        score = jnp.concatenate([score, jnp.full((tq, LANES - ns), PAD_SCORE, F32)], axis=1)
        ns8 = -(-ns // 8) * 8
        sel_t = _select_top(score.T[0:ns8], ns, n_sel)
        blk_t = lax.broadcasted_iota(jnp.int32, (ns8, tq), 0).astype(F32)
        low = jnp.where(sel_t > 0.5, jnp.where(blk_t >= float(tk // SLC_BLOCK), blk_t, float(ns8)),
                        float(ns8))
        lowest = low if lowest is None else jnp.minimum(lowest, low)
        sel = jnp.concatenate([sel_t, jnp.zeros((LANES - ns8, tq), F32)], axis=0).T
        sel = jnp.concatenate([sel] * NSA_G, axis=0)

        feat_w = qw_scr[h, :, LANES:2 * LANES].astype(F32)
        feat_s = jnp.where(lane_m < ns, jnp.where(sel > 0.5, feat_w, -MASK_BIAS), feat_w)
        qs_scr[h] = jnp.concatenate([qb, feat_s.astype(BF16)], axis=1)

    for h in range(NSA_KVH):
        tile(SEL, h, i, ks_ref, vs_ref, qs_scr, "causal", None)

    lo = jnp.minimum((jnp.min(lowest) * (SLC_BLOCK / tk)).astype(jnp.int32), i)
    n_mid = i - lo
    n_tiles = n_mid + jnp.where(i >= 1, 1, 0)

    def sel_body(jj, carry):
        ia = 2 * jj
        ib = ia + 1
        ja = jnp.where(ia < n_mid, i - 1 - ia, 0)
        jb = jnp.where(ib < n_mid, i - 1 - ib, 0)
        pen_b = jnp.where(ib < n_tiles, 0.0, NEG)
        for h in range(NSA_KVH):
            tile(SEL, h, ja, ks_ref, vs_ref, qs_scr, None, None)
        for h in range(NSA_KVH):
            tile(SEL, h, jb, ks_ref, vs_ref, qs_scr, None, pen_b)
        return carry

    lax.fori_loop(0, (n_tiles + 1) // 2, sel_body, 0)
    o_s = [finish(SEL, h) for h in range(NSA_KVH)]
    o_w = [finish(WIN, h) for h in range(NSA_KVH)]

    gates = g_ref[0]
    out_cols = []
    for h in range(NSA_KVH):
        o_c = oc_scr[h]
        for g in range(NSA_G):
            col = GATE_COL0 + (h * NSA_G + g) * 3
            rs = slice(g * tq, (g + 1) * tq)
            hs = slice(h * NSA_DH, (h + 1) * NSA_DH)
            out_cols.append(gates[:, col:col + 1] * o_c[rs, hs]
                            + gates[:, col + 1:col + 2] * o_s[h][rs, hs]
                            + gates[:, col + 2:col + 3] * o_w[h][rs, hs])
    o_ref[0] = jnp.concatenate(out_cols, axis=1)


def _alibi_slopes():
    h = np.arange(1, NSA_HEADS + 1, dtype=np.float32)
    return np.exp2(-8.0 * h / NSA_HEADS).astype(np.float32).reshape(NSA_KVH, NSA_G)


def _nsa_prompt(qn, kcvc, rows, winr, gts):
    bsz, t, _ = qn.shape
    tq = min(t, 256)
    assert t % tq == 0 and WINDOW % tq == 0 and tq % SLC_BLOCK == 0 and tq % LANES == 0
    nch = kcvc.shape[2]
    ns = -(-t // SLC_BLOCK)
    assert ns < LANES
    n_sel = min(SLC_TOPN, ns)
    m_rows = NSA_G * tq
    slopes = _alibi_slopes()
    qi = np.tile(np.arange(tq, dtype=np.float32), NSA_G)
    slope_rows = np.repeat(slopes, tq, axis=1)
    slope_mat = np.ascontiguousarray(np.broadcast_to(slope_rows[:, :, None], (NSA_KVH, m_rows, LANES)))
    dq_mat = qi[:, None] - np.arange(tq, dtype=np.float32)[None, :]
    assert nch <= 2 * LANES
    cfeat = np.zeros((nch, LANES), np.float32)
    cfeat[:, 0] = np.arange(nch)
    cmp_end = np.arange(nch, dtype=np.float32) * CMP_STRIDE + (CMP_LEN - 1)
    dc_mat = qi[:, None] - cmp_end[None, :]
    mmap = jnp.asarray(_cmp_to_slc_map(nch, ns), BF16)
    key = np.arange(t)
    kfeat = np.zeros((t, LANES), np.float32)
    kfeat[key, key // SLC_BLOCK] = 1.0
    kfeat[:, ns] = key % SLC_BLOCK

    kv = lambda c: pl.BlockSpec((1, t, KV_LANES), lambda b, i: (b, 0, c))
    cst = lambda a: pl.BlockSpec(a.shape, lambda b, i: (0,) * a.ndim)
    consts = [jnp.asarray(slope_mat), jnp.asarray(dq_mat), jnp.asarray(cfeat, BF16),
              jnp.asarray(dc_mat), mmap, jnp.asarray(kfeat, BF16)]
    return pl.pallas_call(
        functools.partial(_nsa_prompt_body, tq=tq, ns=ns, n_sel=n_sel),
        grid=(bsz, t // tq),
        in_specs=[pl.BlockSpec((1, tq, NSA_Q), lambda b, i: (b, i, 0)),
                  pl.BlockSpec((1, 1, nch, KV_LANES), lambda b, i: (0, b, 0, 0)),
                  pl.BlockSpec((1, 1, nch, KV_LANES), lambda b, i: (1, b, 0, 0)),
                  kv(2), kv(3), kv(0), kv(1),
                  pl.BlockSpec((1, tq, LANES), lambda b, i: (b, i, 0))] + [cst(a) for a in consts],
        out_specs=pl.BlockSpec((1, tq, NSA_Q), lambda b, i: (b, i, 0)),
        out_shape=_sds((bsz, t, NSA_Q)),
        scratch_shapes=[pltpu.VMEM((NSA_KVH, m_rows, 2 * LANES), BF16),
                        pltpu.VMEM((NSA_KVH, m_rows, 2 * LANES), BF16),
                        pltpu.VMEM((2, NSA_KVH, m_rows, LANES), F32),
                        pltpu.VMEM((2, NSA_KVH, m_rows, KV_LANES), F32),
                        pltpu.VMEM((NSA_KVH, m_rows, KV_LANES), F32)],
        compiler_params=_params(("arbitrary", "arbitrary")),
        name="nsa_prompt",
    )(qn, kcvc, kcvc, rows, rows, winr, winr, gts, *consts)


def _nsa_sample_body(pt_ref, q_ref, kc_ref, vc_ref, rows_ref, winr_ref, swt_ref, g_ref,
                     slope_ref, pos_ref, cur_ref, dc_ref, gsum_ref, mmap_ref, expand_ref,
                     kfeat_ref, cache_ref, o_ref, wout_ref, kvbuf, sem, *, past, t, npages, ns, n_sel):
    b = pl.program_id(0)
    nb = pl.num_programs(0)
    page = past // npages
    m_rows = NSA_HEADS * t
    half = m_rows // NSA_KVH
    pw = swt_ref.shape[-1]

    def page_copy(bb, slot, p):
        return pltpu.make_async_copy(cache_ref.at[pt_ref[bb, p], pl.ds(2, 2)],
                                     kvbuf.at[slot, :, :, pl.ds(p * page, page)], sem.at[slot])

    @pl.when(b == 0)
    def _():
        for p in range(npages):
            page_copy(0, 0, p).start()

    @pl.when(b + 1 < nb)
    def _():
        for p in range(npages):
            page_copy(b + 1, (b + 1) % 2, p).start()

    pieces = []
    for hh in range(NSA_HEADS):
        qg = q_ref[0, :, hh * NSA_DH:(hh + 1) * NSA_DH]
        z = jnp.zeros_like(qg)
        pieces.append(jnp.concatenate([qg, z] if hh < NSA_G else [z, qg], axis=1))
    qbd = jnp.concatenate(pieces, axis=0).astype(BF16)
    slope = slope_ref[...]
    pos = pos_ref[...]

    def widen(x, n):
        return jnp.concatenate([x] * (n // LANES), axis=1)

    def softmax(parts):
        mx = None
        for s in parts:
            r = jnp.max(s, axis=-1, keepdims=True)
            mx = r if mx is None else jnp.maximum(mx, r)
        ps = [jnp.where(s > 0.5 * NEG, jnp.exp(s - mx), 0.0) for s in parts]
        tot = None
        for p in ps:
            r = jnp.sum(p, axis=-1, keepdims=True)
            tot = r if tot is None else tot + r
        inv = 1.0 / jnp.maximum(tot, 1e-30)
        return [p * inv for p in ps]

    d_c = dc_ref[...]
    nch = d_c.shape[1]
    s_c = _dot_nt(qbd, kc_ref[0, 0])
    (p_c,) = softmax([jnp.where(d_c >= 0.0, s_c - widen(slope, nch) * d_c, NEG)])
    o_c = _dot(p_c, vc_ref[0, 0])

    imp = _dot_exact_lhs(gsum_ref[...], _dot_exact_rhs(p_c, mmap_ref[...]))
    ns8 = imp.shape[1]
    imp = jnp.concatenate([imp, jnp.zeros((m_rows, 2 * LANES - ns8), F32)], axis=1)
    imp = jnp.concatenate([imp, jnp.zeros((LANES - m_rows, 2 * LANES), F32)], axis=0)
    imp_t = imp.T[0:ns8, 0:m_rows]
    blk = lax.broadcasted_iota(jnp.int32, (ns8, m_rows), 0)
    cur = cur_ref[...]
    forced = (blk == 0) | (blk == cur) | (blk == cur - 1)
    score = jnp.where(blk <= cur, jnp.where(forced, SEL_FORCE, imp_t), -SEL_FORCE)
    score = jnp.where(blk < ns, score, PAD_SCORE)
    sel_t = _select_top(score, ns, n_sel)
    sel_t = jnp.concatenate([sel_t, jnp.zeros((ns8, LANES - m_rows), F32)], axis=1)
    sel_t = jnp.concatenate([sel_t, jnp.zeros((2 * LANES - ns8, LANES), F32)], axis=0)
    sel = sel_t.T[0:m_rows, 0:ns8]

    new_ks = rows_ref[0, :, 2 * KV_LANES:3 * KV_LANES]
    new_vs = rows_ref[0, :, 3 * KV_LANES:4 * KV_LANES]
    new_kw = winr_ref[0, :, 0:KV_LANES]
    new_vw = winr_ref[0, :, KV_LANES:2 * KV_LANES]
    dist_n = pos[:, 0:t] - (past + lax.broadcasted_iota(jnp.int32, (m_rows, t), 1)).astype(F32)
    slope_n = slope[:, 0:t]

    kwt = swt_ref[0, 0]
    vwt = swt_ref[0, 1]
    dist_w = widen(pos, pw) - (lax.broadcasted_iota(jnp.int32, (m_rows, pw), 1)
                               + (past - pw)).astype(F32)
    s_w = _dot(qbd, kwt)
    s_w = jnp.where(dist_w >= 0.0,
                    jnp.where(dist_w <= float(WINDOW), s_w - widen(slope, pw) * dist_w, NEG), NEG)
    s_wn = _dot_nt(qbd, new_kw)
    s_wn = jnp.where(dist_n >= 0.0,
                     jnp.where(dist_n <= float(WINDOW), s_wn - slope_n * dist_n, NEG), NEG)
    p_w, p_wn = softmax([s_w, s_wn])
    o_w = _dot_nt(p_w, vwt) + _dot(p_wn, new_vw)
    w_rows = jnp.concatenate([jnp.concatenate([kwt.T, vwt.T], axis=1),
                              winr_ref[0]], axis=0)
    wout_ref[0] = w_rows[pw + t - wout_ref.shape[1]:, :]

    slot = b % 2
    for p in range(npages):
        page_copy(b, slot, p).wait()
    nfull = past // SLC_BLOCK
    lane_f = lax.broadcasted_iota(jnp.int32, (m_rows, 16), 1)
    slope_f = slope[:, 0:16]
    q_feat = jnp.where(lane_f == 0, slope_f * float(SLC_BLOCK),
                       jnp.where(lane_f == 1, slope_f, 0.0)).astype(BF16)
    sel_bias = jnp.where(sel > 0.5, 0.0, -MASK_BIAS).astype(BF16)
    s_s = (_dot(qbd, kvbuf[slot, 0])
           + jnp.dot(sel_bias, expand_ref[...], preferred_element_type=F32)
           + jnp.dot(q_feat, kfeat_ref[...], preferred_element_type=F32))
    row_n = lax.broadcasted_iota(jnp.int32, (m_rows, t), 1).astype(F32)
    s_sn = _dot_nt(qbd, new_ks)
    s_sn = jnp.where(sel[:, nfull:nfull + 1] > 0.5,
                     jnp.where(dist_n >= 0.0, s_sn + slope_n * row_n, NEG), NEG)
    p_s, p_sn = softmax([s_s, s_sn])
    o_s = _dot_nt(p_s, kvbuf[slot, 1]) + _dot(p_sn, new_vs)

    def own(x):
        return jnp.concatenate([x[0:half, 0:NSA_DH], x[half:, NSA_DH:2 * NSA_DH]], axis=0)

    gates = g_ref[0]

    def gate_col(br):
        return jnp.concatenate(
            [gates[:, GATE_COL0 + hh * 3 + br:GATE_COL0 + hh * 3 + br + 1] for hh in range(NSA_HEADS)],
            axis=0)

    mix = gate_col(0) * own(o_c) + gate_col(1) * own(o_s) + gate_col(2) * own(o_w)
    o_ref[0] = jnp.concatenate([mix[hh * t:(hh + 1) * t, :] for hh in range(NSA_HEADS)], axis=1)


def _nsa_sample(qn, kcvc, rows, winr, state_win_t, gts, cache_t, page_table):
    bsz, t, _ = qn.shape
    npages = page_table.shape[1]
    page = cache_t.shape[-1]
    past = npages * page
    pw = state_win_t.shape[-1]
    nch = kcvc.shape[2]
    ns = -(-(past + t) // SLC_BLOCK)
    ns8 = -(-ns // 8) * 8
    m_rows = NSA_HEADS * t
    assert past % SLC_BLOCK == 0 and t <= SLC_BLOCK and t % 8 == 0 and pw + t >= WINDOW
    assert m_rows <= LANES and ns8 <= 2 * LANES and nch % LANES == 0 and pw % LANES == 0
    n_sel = min(SLC_TOPN, ns)
    slopes = _alibi_slopes().reshape(-1)
    lanes1 = np.ones((1, LANES), np.float32)
    slope_b = np.repeat(slopes, t)[:, None].astype(np.float32) * lanes1
    pos_i = np.tile(past + np.arange(t), NSA_HEADS)
    cmp_end = np.arange(nch, dtype=np.float32) * CMP_STRIDE + (CMP_LEN - 1)
    dc_mat = pos_i[:, None].astype(np.float32) - cmp_end[None, :]
    row_kvh = np.arange(m_rows) // (NSA_G * t)
    row_tok = np.arange(m_rows) % t
    gsum = ((row_kvh[:, None] == row_kvh[None, :]) & (row_tok[:, None] == row_tok[None, :]))
    mmap = np.zeros((nch, ns8), np.float32)
    mmap[:, :ns] = _cmp_to_slc_map(nch, ns)
    expand = (np.arange(ns8)[:, None] == (np.arange(past) // SLC_BLOCK)[None, :])
    assert past // SLC_BLOCK <= 2 * LANES
    kfeat = np.zeros((16, past), np.float32)
    kfeat[0] = np.arange(past) // SLC_BLOCK - past // SLC_BLOCK
    kfeat[1] = np.arange(past) % SLC_BLOCK
    consts = [jnp.asarray(slope_b), jnp.asarray(pos_i[:, None].astype(np.float32) * lanes1),
              jnp.asarray((pos_i // SLC_BLOCK)[None, :].astype(np.int32)), jnp.asarray(dc_mat),
              jnp.asarray(gsum.astype(np.float32), BF16), jnp.asarray(mmap, BF16),
              jnp.asarray(expand.astype(np.float32), BF16), jnp.asarray(kfeat, BF16)]

    cst = lambda a: pl.BlockSpec(a.shape, lambda b, pt: (0,) * a.ndim)
    grid_spec = pltpu.PrefetchScalarGridSpec(
        num_scalar_prefetch=1,
        grid=(bsz,),
        in_specs=[pl.BlockSpec((1, t, NSA_Q), lambda b, pt: (b, 0, 0)),
                  pl.BlockSpec((1, 1, nch, KV_LANES), lambda b, pt: (0, b, 0, 0)),
                  pl.BlockSpec((1, 1, nch, KV_LANES), lambda b, pt: (1, b, 0, 0)),
                  pl.BlockSpec((1, t, ROWS_LANES), lambda b, pt: (b, 0, 0)),
                  pl.BlockSpec((1, t, 2 * KV_LANES), lambda b, pt: (b, 0, 0)),
                  pl.BlockSpec((1, 2, KV_LANES, pw), lambda b, pt: (b, 0, 0, 0)),
                  pl.BlockSpec((1, t, LANES), lambda b, pt: (b, 0, 0))]
        + [cst(a) for a in consts] + [pl.BlockSpec(memory_space=pl.ANY)],
        out_specs=[pl.BlockSpec((1, t, NSA_Q), lambda b, pt: (b, 0, 0)),
                   pl.BlockSpec((1, WINDOW, 2 * KV_LANES), lambda b, pt: (b, 0, 0))],
        scratch_shapes=[pltpu.VMEM((2, 2, KV_LANES, past), F32), pltpu.SemaphoreType.DMA((2,))],
    )
    return pl.pallas_call(
        functools.partial(_nsa_sample_body, past=past, t=t, npages=npages, ns=ns, n_sel=n_sel),
        grid_spec=grid_spec,
        out_shape=[_sds((bsz, t, NSA_Q)), _sds((bsz, WINDOW, 2 * KV_LANES))],
        compiler_params=_params(("arbitrary",)),
        name="nsa_sample",
    )(page_table, qn, kcvc, kcvc, rows, winr, state_win_t, gts, *consts, cache_t)


def _ffn_body(x_ref, og_ref, on_ref, gtm_ref, scf_ref, shf_ref, gtf_ref,
              nmp_ref, nfp_ref, nfo_ref, wo_ref, wu_ref, wd_ref, y_ref, *, ff_chunk):
    mix = (jnp.dot(og_ref[0].astype(BF16), wo_ref[0:GLA_V, :], preferred_element_type=F32)
           + jnp.dot(on_ref[0].astype(BF16), wo_ref[GLA_V:GLA_V + NSA_Q, :],
                     preferred_element_type=F32))
    x1 = x_ref[0] + gtm_ref[0] * _rms(mix, nmp_ref[...])
    hb = (_rms(x1, nfp_ref[...]) * (1.0 + scf_ref[0]) + shf_ref[0]).astype(BF16)
    f = None
    for c in range(wu_ref.shape[1] // ff_chunk):
        cs = slice(c * ff_chunk, (c + 1) * ff_chunk)
        u = jnp.maximum(jnp.dot(hb, wu_ref[:, cs], preferred_element_type=F32), 0.0)
        term = jnp.dot((u * u).astype(BF16), wd_ref[cs, :], preferred_element_type=F32)
        f = term if f is None else f + term
    y_ref[0] = x1 + gtf_ref[0] * _rms(f, nfo_ref[...])


def _ffn(x, og, on, gtm, scf, shf, gtf, nmp, nfp, nfo, wo, wu, wd):
    bsz, t, d = x.shape
    tq = min(t, 512)
    assert t % tq == 0
    tok = lambda n: pl.BlockSpec((1, tq, n), lambda b, i: (b, i, 0))
    full = lambda a: pl.BlockSpec(a.shape, lambda b, i: (0,) * a.ndim,
                                  pipeline_mode=pl.Buffered(1))
    vec = lambda a: pl.BlockSpec(a.shape, lambda b, i: (0,) * a.ndim)
    return pl.pallas_call(
        functools.partial(_ffn_body, ff_chunk=min(wu.shape[1], 1024)),
        grid=(bsz, t // tq),
        in_specs=[tok(d), tok(GLA_V), tok(NSA_Q), _mod_spec(gtm, tq), _mod_spec(scf, tq),
                  _mod_spec(shf, tq), _mod_spec(gtf, tq), vec(nmp), vec(nfp), vec(nfo),
                  full(wo), full(wu), full(wd)],
        out_specs=tok(d),
        out_shape=_sds((bsz, t, d)),
        compiler_params=_params(("arbitrary", "arbitrary")),
        name="ffn",
    )(x, og, on, gtm, scf, shf, gtf, nmp, nfp, nfo, wo, wu, wd)


def _permute_w_in(w_in):
    o_zq = 0
    o_zk = o_zq + GLA_QK
    o_zv = o_zk + GLA_QK
    o_za = o_zv + GLA_V
    o_zr = o_za + GLA_RANK
    o_zqn = o_zr + GLA_V
    o_zkv = o_zqn + NSA_Q
    o_zg = o_zkv + NSA_KV
    o_end = o_zg + NSA_GATE
    d = w_in.shape[0]
    pad = jnp.zeros((d, LANES - GLA_RANK - NSA_GATE), w_in.dtype)
    cols = [w_in[:, o_zq:o_za], w_in[:, o_zr:o_zg], w_in[:, o_za:o_zr], w_in[:, o_zg:o_end], pad]
    return jnp.concatenate(cols, axis=1).astype(BF16)


def _layer_weights(l, norm_mix_pre, norm_mix_post, norm_ffn_pre, norm_ffn_post, w_in,
                   gla_w_gate, gla_b_gate, gla_norm, cmp_pos, cmp_w1, cmp_b1, cmp_w2, cmp_b2,
                   w_out, w_up, w_down):
    wg_pad = jnp.zeros((LANES, GLA_QK), F32).at[:GLA_RANK].set(gla_w_gate[l]).astype(BF16)
    w1, w2 = _compress_weights(cmp_w1[l], cmp_w2[l])
    return dict(
        nmpre=norm_mix_pre[l][None, :], nmpost=norm_mix_post[l][None, :],
        nfpre=norm_ffn_pre[l][None, :], nfpost=norm_ffn_post[l][None, :],
        w_perm=_permute_w_in(w_in[l]), wg_pad=wg_pad, bg=gla_b_gate[l][None, :],
        gnorm=gla_norm[l][None, :], w1=w1, w2=w2, posb=_posbias(cmp_pos[l], cmp_w1[l]),
        b1=cmp_b1[l][:, None, :], b2=cmp_b2[l][:, None, :],
        wo=w_out[l].astype(BF16), wu=w_up[l].astype(BF16), wd=w_down[l].astype(BF16))


def _split_ada(ada, rows_per_batch):
    parts = jnp.split(ada, 6, axis=-1)
    if rows_per_batch is None:
        return [p[:, None, :] for p in parts]
    return [jnp.repeat(p, rows_per_batch, axis=0)[None] for p in parts]


def _prompt_layer(x, ada, w, wbuf):
    bsz, t, d = x.shape
    sh_m, sc_m, gt_m, sh_f, sc_f, gt_f = _split_ada(ada, None)
    qg, kg, vg, rg, qn, rows, winr, logf, gts = _inproj(x, sc_m, sh_m, w["nmpre"], w["w_perm"],
                                                        w["wg_pad"], w["bg"])
    s0 = jnp.zeros((bsz, GLA_HEADS, GLA_DK, GLA_DV), F32)
    og, s_fin = _gla(qg, kg, vg, logf, rg, s0, w["gnorm"])
    kcvc = _compress_prompt(rows, w["w1"], w["posb"], w["b1"], w["w2"], w["b2"])
    on = _nsa_prompt(qn, kcvc, rows, winr, gts)
    y = _ffn(x, og, on, gt_m, sc_f, sh_f, gt_f, w["nmpost"], w["nfpre"], w["nfpost"],
             w["wo"], w["wu"], w["wd"])
    win = winr.reshape(bsz, t, 2, NSA_KVH, NSA_DH)
    if t < wbuf:
        win = jnp.pad(win, ((0, 0), (wbuf - t, 0), (0, 0), (0, 0), (0, 0)))
    return y, rows.reshape(bsz, t, 4, NSA_KVH, NSA_DH), win[:, -wbuf:], s_fin


def _sample_layer(x, ada, cache, page_table, state_win, state_gla, w):
    bsz, t, d = x.shape
    n = bsz * t
    sh_m, sc_m, gt_m, sh_f, sc_f, gt_f = _split_ada(ada, t)
    outs = _inproj(x.reshape(1, n, d), sc_m, sh_m, w["nmpre"], w["w_perm"], w["wg_pad"], w["bg"])
    qg, kg, vg, rg, qn, rows, winr, logf, gts = [o.reshape(bsz, t, o.shape[-1]) for o in outs]
    og, s_fin = _gla(qg, kg, vg, logf, rg, state_gla, w["gnorm"])
    n_pool, page = cache.shape[0], cache.shape[1]
    cache_t = jnp.transpose(cache, (0, 2, 3, 4, 1)).reshape(n_pool, 4, KV_LANES, page)
    pw = state_win.shape[1]
    state_win_t = jnp.transpose(state_win, (0, 2, 3, 4, 1)).reshape(bsz, 2, KV_LANES, pw)
    kcvc = _compress_paged(cache_t, page_table, w["w1"], w["posb"], w["b1"], w["w2"], w["b2"])
    on, win = _nsa_sample(qn, kcvc, rows, winr, state_win_t, gts, cache_t, page_table)
    y = _ffn(x.reshape(1, n, d), og.reshape(1, n, GLA_V), on.reshape(1, n, NSA_Q),
             gt_m, sc_f, sh_f, gt_f, w["nmpost"], w["nfpre"], w["nfpost"],
             w["wo"], w["wu"], w["wd"])
    return (y.reshape(bsz, t, d), rows.reshape(bsz, t, 4, NSA_KVH, NSA_DH),
            win.reshape(bsz, WINDOW, 2, NSA_KVH, NSA_DH), s_fin)


def kernel(x_prompt, x_sample, cache_kv, state_win, state_gla, page_table, c_prompt, c_sample,
           norm_mix_pre, norm_mix_post, norm_ffn_pre, norm_ffn_post, w_ada, b_ada, w_in,
           gla_w_gate, gla_b_gate, gla_norm, cmp_pos, cmp_w1, cmp_b1, cmp_w2, cmp_b2,
           w_out, w_up, w_down):
    depth = w_in.shape[0]
    bsz = x_prompt.shape[0]
    wbuf = state_win.shape[2]
    assert wbuf == WINDOW
    c_all = jnp.concatenate([c_prompt, c_sample], axis=0)
    y_p, y_s = x_prompt, x_sample
    outs = [[] for _ in range(6)]
    for l in range(depth):
        w = _layer_weights(l, norm_mix_pre, norm_mix_post, norm_ffn_pre, norm_ffn_post, w_in,
                           gla_w_gate, gla_b_gate, gla_norm, cmp_pos, cmp_w1, cmp_b1, cmp_w2,
                           cmp_b2, w_out, w_up, w_down)
        ada = _ada(c_all, w_ada[l], b_ada[l])
        y_p, r_p, w_p, s_p = _prompt_layer(y_p, ada[:bsz], w, wbuf)
        y_s, r_s, w_s, s_s = _sample_layer(y_s, ada[bsz:], cache_kv[l], page_table,
                                           state_win[l], state_gla[l], w)
        for lst, val in zip(outs, (r_p, r_s, w_p, w_s, s_p, s_s)):
            lst.append(val)
    return (y_p, y_s) + tuple(jnp.stack(o) for o in outs)
```

```python
import functools
import math

import numpy as np
import jax
import jax.numpy as jnp
from jax import lax
from jax.experimental import pallas as pl
from jax.experimental.pallas import tpu as pltpu

F32 = jnp.float32
BF16 = jnp.bfloat16

GLA_HEADS = 4
GLA_DK = 64
GLA_DV = 128
GLA_RANK = 16
GLA_NORMALIZER = 16.0
GLA_CHUNK = 64
NSA_HEADS = 8
NSA_DH = 64
NSA_KVH = 2
NSA_G = NSA_HEADS // NSA_KVH
CMP_STRIDE = 16
CMP_LEN = 32
CMP_HIDDEN = 256
SLC_BLOCK = 64
SLC_TOPN = 16
SEL_FORCE = 1.0e4
WINDOW = 512
EPS = 1e-6
NEG = -1.0e30
PAD_SCORE = -3.0e4
MASK_BIAS = 2.0 ** 60
LANES = 128
VMEM_LIMIT = 56 * 1024 * 1024

GLA_QK = GLA_HEADS * GLA_DK
GLA_V = GLA_HEADS * GLA_DV
NSA_Q = NSA_HEADS * NSA_DH
NSA_KV = 6 * NSA_KVH * NSA_DH
NSA_GATE = 3 * NSA_HEADS
KV_LANES = NSA_KVH * NSA_DH
ROWS_LANES = 4 * KV_LANES


def _sds(shape, dtype=F32):
    return jax.ShapeDtypeStruct(shape, dtype)


def _params(sem):
    return pltpu.CompilerParams(dimension_semantics=sem, vmem_limit_bytes=VMEM_LIMIT)


def _dot(a, b):
    return jnp.dot(a.astype(BF16), b.astype(BF16), preferred_element_type=F32)


def _dot_nt(a, b):
    return lax.dot_general(a.astype(BF16), b.astype(BF16), (((1,), (1,)), ((), ())),
                           preferred_element_type=F32)


def _dot_tn(a, b):
    return lax.dot_general(a.astype(BF16), b.astype(BF16), (((0,), (0,)), ((), ())),
                           preferred_element_type=F32)


def _split3(x):
    p1 = x.astype(BF16)
    r1 = x - p1.astype(F32)
    p2 = r1.astype(BF16)
    p3 = (r1 - p2.astype(F32)).astype(BF16)
    return p1, p2, p3


def _dot_f32(a, b):
    a1, a2, _ = _split3(a)
    b1, b2, _ = _split3(b)
    return (jnp.dot(a1, b1, preferred_element_type=F32)
            + jnp.dot(a1, b2, preferred_element_type=F32)
            + jnp.dot(a2, b1, preferred_element_type=F32))


def _dot_exact_lhs(a_bf16, x):
    out = None
    for p in _split3(x):
        t = jnp.dot(a_bf16, p, preferred_element_type=F32)
        out = t if out is None else out + t
    return out


def _dot_exact_rhs(x, b_bf16):
    out = None
    for p in _split3(x):
        t = jnp.dot(p, b_bf16, preferred_element_type=F32)
        out = t if out is None else out + t
    return out


def _sigmoid(x):
    return 1.0 / (1.0 + jnp.exp(-x))


def _silu(x):
    return x * _sigmoid(x)


def _rms(x, g):
    return x * lax.rsqrt(jnp.mean(x * x, axis=-1, keepdims=True) + EPS) * g


def _ada_body(c_ref, w_ref, b_ref, o_ref):
    o_ref[...] = _dot_f32(_silu(c_ref[...]), w_ref[...]) + b_ref[...]


def _ada(c, w_ada, b_ada):
    n, d = c.shape
    nout = w_ada.shape[1]
    tn = d
    return pl.pallas_call(
        _ada_body,
        grid=(nout // tn,),
        in_specs=[pl.BlockSpec((n, d), lambda j: (0, 0)),
                  pl.BlockSpec((d, tn), lambda j: (0, j)),
                  pl.BlockSpec((1, tn), lambda j: (0, j))],
        out_specs=pl.BlockSpec((n, tn), lambda j: (0, j)),
        out_shape=_sds((n, nout)),
        compiler_params=_params(("arbitrary",)),
        name="ada",
    )(c, w_ada, b_ada.reshape(1, nout))


_C_ZQ = 0
_C_ZK = _C_ZQ + GLA_QK
_C_ZV = _C_ZK + GLA_QK
_C_ZR = _C_ZV + GLA_V
_C_ZQN = _C_ZR + GLA_V
_C_ZKV = _C_ZQN + NSA_Q
_C_MISC = _C_ZKV + NSA_KV
_C_END = _C_MISC + LANES
GATE_COL0 = GLA_RANK


def _inproj_body(x_ref, sc_ref, sh_ref, g_ref, w_ref, wg_ref, bg_ref,
                 qg_ref, kg_ref, vg_ref, rg_ref, qn_ref, rows_ref, winr_ref, logf_ref, gts_ref):
    h = _rms(x_ref[0], g_ref[...]) * (1.0 + sc_ref[0]) + sh_ref[0]
    hb = h.astype(BF16)

    def proj(a, b):
        return jnp.dot(hb, w_ref[:, a:b], preferred_element_type=F32)

    qg_ref[0] = proj(_C_ZQ, _C_ZK) * (GLA_DK ** -0.5)
    kg_ref[0] = proj(_C_ZK, _C_ZV)
    vg_ref[0] = proj(_C_ZV, _C_ZR)
    rg_ref[0] = _silu(proj(_C_ZR, _C_ZQN))
    qn_ref[0] = proj(_C_ZQN, _C_ZKV) * (NSA_DH ** -0.5)
    rows_ref[0] = proj(_C_ZKV, _C_ZKV + ROWS_LANES)
    winr_ref[0] = proj(_C_ZKV + ROWS_LANES, _C_MISC)
    zm = proj(_C_MISC, _C_END)
    gts_ref[0] = _sigmoid(zm)
    logit = jnp.dot(zm.astype(BF16), wg_ref[...], preferred_element_type=F32) + bg_ref[...]
    log_sig = jnp.minimum(logit, 0.0) - jnp.log(1.0 + jnp.exp(-jnp.abs(logit)))
    logf_ref[0] = log_sig * (1.0 / GLA_NORMALIZER)


def _mod_spec(mod, tq):
    d = mod.shape[-1]
    if mod.shape[1] == 1:
        return pl.BlockSpec((1, 1, d), lambda b, t: (b, 0, 0))
    return pl.BlockSpec((1, tq, d), lambda b, t: (b, t, 0))


def _inproj(x, sc, sh, gain, w_perm, wg_pad, bg):
    bsz, t, d = x.shape
    tq = min(t, 512)
    assert t % tq == 0
    widths = (GLA_QK, GLA_QK, GLA_V, GLA_V, NSA_Q, ROWS_LANES, NSA_KV - ROWS_LANES, GLA_QK, LANES)
    tok = lambda n: pl.BlockSpec((1, tq, n), lambda b, i: (b, i, 0))
    full = lambda a: pl.BlockSpec(a.shape, lambda b, i: (0,) * a.ndim)
    return pl.pallas_call(
        _inproj_body,
        grid=(bsz, t // tq),
        in_specs=[tok(d), _mod_spec(sc, tq), _mod_spec(sh, tq), full(gain), full(w_perm),
                  full(wg_pad), full(bg)],
        out_specs=[tok(n) for n in widths],
        out_shape=[_sds((bsz, t, n)) for n in widths],
        compiler_params=_params(("arbitrary", "arbitrary")),
        name="inproj",
    )(x, sc, sh, gain, w_perm, wg_pad, bg)


def _head_stack(x, head_of_lane):
    return jnp.concatenate([jnp.where(head_of_lane == h, x, 0.0) for h in range(GLA_HEADS)],
                           axis=0)


def _gla_body(q_ref, k_ref, v_ref, f_ref, r_ref, s0_ref, gn_ref, o_ref, sfin_ref, s_scr,
              *, rows, nchunks, nb):
    c_len = GLA_CHUNK
    t = pl.program_id(1)

    @pl.when(t == 0)
    def _():
        s_scr[...] = s0_ref[...]

    n = nchunks * c_len

    def load(ref, bb):
        x = ref[bb]
        if rows < n:
            x = jnp.concatenate([x, jnp.zeros((n - rows, x.shape[1]), F32)], axis=0)
        return x

    ri = lax.broadcasted_iota(jnp.int32, (n, n), 0)
    ci = lax.broadcasted_iota(jnp.int32, (n, n), 1)
    shift = c_len.bit_length() - 1
    same_chunk = lax.shift_right_logical(ri, shift) == lax.shift_right_logical(ci, shift)
    tril = jnp.where(ci <= ri, jnp.where(same_chunk, 1.0, 0.0), 0.0).astype(BF16)
    causal = (lax.broadcasted_iota(jnp.int32, (c_len, c_len), 1)
              <= lax.broadcasted_iota(jnp.int32, (c_len, c_len), 0))
    head_of_lane = lax.broadcasted_iota(jnp.int32, (c_len, GLA_QK), 1) // GLA_DK

    seqs = []
    for bb in range(nb):
        q, k, v, g, r = (load(ref, bb) for ref in (q_ref, k_ref, v_ref, f_ref, r_ref))
        b = None
        for gp in _split3(g):
            term = jnp.dot(tril, gp, preferred_element_type=F32)
            b = term if b is None else b + term
        seqs.append((q, k, v, r, b, [s_scr[bb, h] for h in range(GLA_HEADS)]))

    for c in range(nchunks):
        cs = slice(c * c_len, (c + 1) * c_len)
        lo = c * c_len
        for bb, (q, k, v, r, b, state) in enumerate(seqs):
            b_c = b[cs]
            b_last = b_c[c_len - 1:c_len, :]
            decay_t = jnp.exp(jnp.broadcast_to(b_last, (GLA_DV, GLA_QK)).T)
            b_mid = b_c[c_len // 2 - 1:c_len // 2, :]
            q_inter = q[cs] * jnp.exp(b_c)
            q_intra = q[cs] * jnp.exp(b_c - b_mid)
            k_intra = k[cs] * jnp.exp(b_mid - b_c)
            k_out = k[cs] * jnp.exp(b_last - b_c)
            v_c = v[cs]
            k_out_t = k_out.T
            a_all = _dot_nt(_head_stack(q_intra, head_of_lane), k_intra)
            o_all = _dot(_head_stack(q_inter, head_of_lane), jnp.concatenate(state, axis=0))
            outs = []
            for h in range(GLA_HEADS):
                ks = slice(h * GLA_DK, (h + 1) * GLA_DK)
                vs = slice(h * GLA_DV, (h + 1) * GLA_DV)
                hs = slice(h * c_len, (h + 1) * c_len)
                a = jnp.where(causal, a_all[hs], 0.0)
                av = _dot(jnp.concatenate([a, k_out_t[ks, :]], axis=0), v_c[:, vs])
                o = av[0:c_len] + o_all[hs]
                state[h] = state[h] * decay_t[ks, :] + av[c_len:]
                outs.append(_rms(o, gn_ref[...]))
            on = jnp.concatenate(outs, axis=1) * r[cs]
            o_ref[bb, lo:min(lo + c_len, rows), :] = on[:min(c_len, rows - lo)]
    for bb, seq in enumerate(seqs):
        for h in range(GLA_HEADS):
            s_scr[bb, h] = seq[5][h]

    @pl.when(t == pl.num_programs(1) - 1)
    def _():
        sfin_ref[...] = s_scr[...]


def _gla(qg, kg, vg, logf, rg, s0, gnorm):
    bsz, t, _ = qg.shape
    tt = min(t, 4 * GLA_CHUNK)
    assert t % tt == 0 and (tt % GLA_CHUNK == 0 or tt < GLA_CHUNK)
    nchunks = max(tt // GLA_CHUNK, 1)
    nb = math.gcd(bsz, 4 if nchunks > 1 else 16)
    tok = lambda n: pl.BlockSpec((nb, tt, n), lambda b, i: (b, i, 0))
    st = pl.BlockSpec((nb, GLA_HEADS, GLA_DK, GLA_DV), lambda b, i: (b, 0, 0, 0))
    return pl.pallas_call(
        functools.partial(_gla_body, rows=tt, nchunks=nchunks, nb=nb),
        grid=(bsz // nb, t // tt),
        in_specs=[tok(GLA_QK), tok(GLA_QK), tok(GLA_V), tok(GLA_QK), tok(GLA_V), st,
                  pl.BlockSpec((1, GLA_DV), lambda b, i: (0, 0))],
        out_specs=[tok(GLA_V), st],
        out_shape=[_sds((bsz, t, GLA_V)), _sds((bsz, GLA_HEADS, GLA_DK, GLA_DV))],
        scratch_shapes=[pltpu.VMEM((nb, GLA_HEADS, GLA_DK, GLA_DV), F32)],
        compiler_params=_params(("arbitrary", "arbitrary")),
        name="gla",
    )(qg, kg, vg, logf, rg, s0, gnorm)


def _posbias_body(p_ref, w_ref, o_ref):
    o_ref[0, 0] = _dot_f32(p_ref[0, 0], w_ref[0, 0])


def _posbias(cmp_pos, cmp_w1):
    kdim = CMP_STRIDE * NSA_DH
    pos = jnp.broadcast_to(cmp_pos.reshape(2, 2, 1, kdim), (2, 2, 8, kdim))
    w = cmp_w1.reshape(2, 2, kdim, CMP_HIDDEN)
    return pl.pallas_call(
        _posbias_body,
        grid=(2, 2),
        in_specs=[pl.BlockSpec((1, 1, 8, kdim), lambda a, b: (a, b, 0, 0)),
                  pl.BlockSpec((1, 1, kdim, CMP_HIDDEN), lambda a, b: (a, b, 0, 0))],
        out_specs=pl.BlockSpec((1, 1, 8, CMP_HIDDEN), lambda a, b: (a, b, 0, 0)),
        out_shape=_sds((2, 2, 8, CMP_HIDDEN)),
        compiler_params=_params(("arbitrary", "arbitrary")),
        name="posbias",
    )(pos, w)


def _compress_rows(x_ref, w1_ref, pb_ref, b1_ref, w2_ref, b2_ref):
    nch = x_ref.shape[0] // CMP_STRIDE
    lo = lax.broadcasted_iota(jnp.int32, (nch, KV_LANES), 1) < NSA_DH
    cols = [[], []]
    for jp in range(CMP_STRIDE // 2):
        xe = x_ref[pl.ds(2 * jp, nch, stride=CMP_STRIDE), :]
        xo = x_ref[pl.ds(2 * jp + 1, nch, stride=CMP_STRIDE), :]
        cols[0].append(jnp.where(lo, xe, pltpu.roll(xo, NSA_DH, axis=1)).astype(BF16))
        cols[1].append(jnp.where(lo, pltpu.roll(xe, NSA_DH, axis=1), xo).astype(BF16))
    bias = pb_ref[0, 0, 0:1, :] + pb_ref[0, 1, 0:1, :] + b1_ref[0]
    outs = []
    for h in range(NSA_KVH):
        z = jnp.dot(jnp.concatenate(cols[h], axis=1), w1_ref[0], preferred_element_type=F32)
        z_second = z[:, CMP_HIDDEN:2 * CMP_HIDDEN]
        z_next = jnp.concatenate([z_second[1:], jnp.zeros((1, CMP_HIDDEN), F32)], axis=0)
        pre = z[:, 0:CMP_HIDDEN] + z_next + bias
        outs.append(_dot(_silu(pre), w2_ref[0]) + b2_ref[0])
    return jnp.concatenate(outs, axis=1)


def _compress_weights(cmp_w1, cmp_w2):
    kdim = CMP_STRIDE * NSA_DH
    w1 = jnp.concatenate([cmp_w1[:, :CMP_STRIDE].reshape(2, kdim, CMP_HIDDEN),
                          cmp_w1[:, CMP_STRIDE:].reshape(2, kdim, CMP_HIDDEN)], axis=-1)
    return w1.astype(BF16), cmp_w2.astype(BF16)


def _compress_wspec(shape):
    nd = len(shape)
    return pl.BlockSpec((1,) + shape[1:], lambda b, cc, *_: (cc,) + (0,) * (nd - 1))


def _compress_prompt_body(x_ref, w1_ref, pb_ref, b1_ref, w2_ref, b2_ref, o_ref):
    o_ref[0, 0] = _compress_rows(x_ref.at[0], w1_ref, pb_ref, b1_ref, w2_ref, b2_ref)


def _compress_prompt(rows, w1, posb, b1, w2, b2):
    bsz, t, _ = rows.shape
    assert t % LANES == 0
    nch = t // CMP_STRIDE
    weights = (w1, posb, b1, w2, b2)
    return pl.pallas_call(
        _compress_prompt_body,
        grid=(bsz, 2),
        in_specs=[pl.BlockSpec((1, t, KV_LANES), lambda b, cc: (b, 0, cc))]
        + [_compress_wspec(a.shape) for a in weights],
        out_specs=pl.BlockSpec((1, 1, nch, KV_LANES), lambda b, cc: (cc, b, 0, 0)),
        out_shape=_sds((2, bsz, nch, KV_LANES)),
        compiler_params=_params(("arbitrary", "arbitrary")),
        name="compress_prompt",
    )(rows, *weights)


def _compress_paged_body(pt_ref, cache_ref, w1_ref, pb_ref, b1_ref, w2_ref, b2_ref, o_ref,
                         xt_buf, x_scr, sem, *, npages):
    b = pl.program_id(0)
    cc = pl.program_id(1)
    step = b * 2 + cc
    nsteps = pl.num_programs(0) * 2
    page = xt_buf.shape[-1]

    def page_copy(bb, c, slot, p):
        return pltpu.make_async_copy(cache_ref.at[pt_ref[bb, p], c], xt_buf.at[slot, p],
                                     sem.at[slot])

    @pl.when(step == 0)
    def _():
        for p in range(npages):
            page_copy(0, 0, 0, p).start()

    @pl.when(step + 1 < nsteps)
    def _():
        nxt = step + 1
        for p in range(npages):
            page_copy(nxt // 2, nxt % 2, nxt % 2, p).start()

    slot = step % 2
    for p in range(npages):
        page_copy(b, cc, slot, p).wait()
    for p in range(npages):
        x_scr[p * page:(p + 1) * page, :] = xt_buf[slot, p].T
    o_ref[0, 0] = _compress_rows(x_scr, w1_ref, pb_ref, b1_ref, w2_ref, b2_ref)


def _compress_paged(cache_t, page_table, w1, posb, b1, w2, b2):
    bsz, npages = page_table.shape
    page = cache_t.shape[-1]
    assert page == LANES and cache_t.shape[2] == KV_LANES
    nch = npages * page // CMP_STRIDE
    weights = (w1, posb, b1, w2, b2)
    grid_spec = pltpu.PrefetchScalarGridSpec(
        num_scalar_prefetch=1,
        grid=(bsz, 2),
        in_specs=[pl.BlockSpec(memory_space=pl.ANY)] + [_compress_wspec(a.shape) for a in weights],
        out_specs=pl.BlockSpec((1, 1, nch, KV_LANES), lambda b, cc, pt: (cc, b, 0, 0)),
        scratch_shapes=[pltpu.VMEM((2, npages, KV_LANES, page), F32),
                        pltpu.VMEM((npages * page, KV_LANES), F32),
                        pltpu.SemaphoreType.DMA((2,))],
    )
    return pl.pallas_call(
        functools.partial(_compress_paged_body, npages=npages),
        grid_spec=grid_spec,
        out_shape=_sds((2, bsz, nch, KV_LANES)),
        compiler_params=_params(("arbitrary", "arbitrary")),
        name="compress_paged",
    )(page_table, cache_t, *weights)


def _cmp_to_slc_map(nc, ns):
    start = np.arange(nc) * CMP_STRIDE
    bs = np.arange(ns) * SLC_BLOCK
    ov = (np.minimum(start[:, None] + CMP_LEN, bs[None, :] + SLC_BLOCK)
          - np.maximum(start[:, None], bs[None, :]))
    return (np.clip(ov, 0, None) / CMP_LEN).astype(np.float32)


def _select_top(score_t, ns, n_sel):
    nrow, nq = score_t.shape
    groups = [score_t[8 * v:8 * v + 8, :] for v in range(nrow // 8)]
    ranks = [jnp.zeros((8, nq), F32) for _ in groups]
    sub = lax.broadcasted_iota(jnp.int32, (8, nq), 0)
    for i in range(ns):
        row = jnp.broadcast_to(score_t[i:i + 1, :], (8, nq))
        for v, grp in enumerate(groups):
            if i < 8 * v:
                ranks[v] = jnp.where(row >= grp, ranks[v] + 1.0, ranks[v])
            elif i >= 8 * v + 8:
                ranks[v] = jnp.where(row > grp, ranks[v] + 1.0, ranks[v])
            else:
                later = jnp.where(row >= grp, 1.0, 0.0)
                earlier = jnp.where(row > grp, 1.0, 0.0)
                ranks[v] = ranks[v] + jnp.where(sub > (i - 8 * v), later, earlier)
    rank = jnp.concatenate(ranks, axis=0)
    return jnp.where(rank < float(n_sel), 1.0, 0.0)


def _nsa_prompt_body(q_ref, kc_ref, vc_ref, ks_ref, vs_ref, kw_ref, vw_ref, g_ref,
                     slope_ref, dq_ref, cfeat_ref, dc_ref, mmap_ref, kfeat_ref,
                     o_ref, qs_scr, qw_scr, m_scr, acc_scr, oc_scr, *, tq, ns, n_sel):
    i = pl.program_id(1)
    t0 = i * tq
    tk = tq
    m_rows = NSA_G * tq
    nwin = WINDOW // tk
    lane_q = lax.broadcasted_iota(jnp.int32, (tq, KV_LANES), 1)
    lane_m = lax.broadcasted_iota(jnp.int32, (m_rows, LANES), 1)
    lane_v = lax.broadcasted_iota(jnp.int32, (tk, KV_LANES), 1)
    dq = dq_ref[...]
    rel_blk = ((lane_m - t0 // SLC_BLOCK) * SLC_BLOCK).astype(F32)
    SEL, WIN = 0, 1

    def tile(br, h, j, k_ref, v_ref, q_scr, mask, pen):
        k0 = pl.multiple_of(j * tk, tk)
        kt = jnp.concatenate([k_ref[0, pl.ds(k0, tk), :].astype(BF16),
                              kfeat_ref[pl.ds(k0, tk), :]], axis=1)
        s = lax.dot_general(q_scr[h], kt, (((1,), (1,)), ((), ())), preferred_element_type=F32)
        if mask == "causal":
            s = jnp.where(dq >= 0.0, s, NEG)
        elif mask == "far":
            s = jnp.where(dq <= 0.0, s, NEG)
        if pen is not None:
            s = s + pen
        vt = jnp.where((lane_v // NSA_DH) == h, v_ref[0, pl.ds(k0, tk), :], 1.0).astype(BF16)
        if mask == "causal":
            m_new = jnp.broadcast_to(jnp.max(s, axis=-1, keepdims=True), (m_rows, LANES))
            p = jnp.exp(s - jnp.concatenate([m_new] * (tk // LANES), axis=1))
            acc_scr[br, h] = jnp.dot(p.astype(BF16), vt, preferred_element_type=F32)
        else:
            m_old = m_scr[br, h]
            m_new = jnp.maximum(m_old, jnp.max(s, axis=-1, keepdims=True))
            p = jnp.exp(s - jnp.concatenate([m_new] * (tk // LANES), axis=1))
            acc_scr[br, h] = (jnp.exp(m_old - m_new) * acc_scr[br, h]
                              + jnp.dot(p.astype(BF16), vt, preferred_element_type=F32))
        m_scr[br, h] = m_new

    def finish(br, h):
        acc = acc_scr[br, h]
        return acc * (1.0 / pltpu.roll(acc, NSA_DH, axis=1))

    qbs = []
    for h in range(NSA_KVH):
        qs = []
        for g in range(NSA_G):
            c0 = (h * NSA_G + g) * NSA_DH
            qg = q_ref[0, :, c0:c0 + NSA_DH]
            qq = jnp.concatenate([qg, qg], axis=1)
            qs.append(jnp.where((lane_q // NSA_DH) == h, qq, 0.0))
        qbs.append(jnp.concatenate(qs, axis=0).astype(BF16))
        slope = slope_ref[h]
        feat_w = jnp.where(lane_m < ns, slope * rel_blk, jnp.where(lane_m == ns, slope, 0.0))
        qw_scr[h] = jnp.concatenate([qbs[h], feat_w.astype(BF16)], axis=1)

    lane_v2 = lax.broadcasted_iota(jnp.int32, (2 * tk, KV_LANES), 1)
    second = lax.broadcasted_iota(jnp.int32, (1, 2 * tk), 1) >= tk

    def tile2(br, ja, jb, k_ref, v_ref, q_scr, pen_b):
        ka = pl.multiple_of(ja * tk, tk)
        kb = pl.multiple_of(jb * tk, tk)
        kt = jnp.concatenate(
            [jnp.concatenate([k_ref[0, pl.ds(k0, tk), :].astype(BF16),
                              kfeat_ref[pl.ds(k0, tk), :]], axis=1) for k0 in (ka, kb)], axis=0)
        v_rows = jnp.concatenate([v_ref[0, pl.ds(ka, tk), :], v_ref[0, pl.ds(kb, tk), :]], axis=0)
        pen = jnp.where(second, pen_b, 0.0)
        for h in range(NSA_KVH):
            s = lax.dot_general(q_scr[h], kt, (((1,), (1,)), ((), ())),
                                preferred_element_type=F32) + pen
            vt = jnp.where((lane_v2 // NSA_DH) == h, v_rows, 1.0).astype(BF16)
            m_old = m_scr[br, h]
            m_new = jnp.maximum(m_old, jnp.max(s, axis=-1, keepdims=True))
            p = jnp.exp(s - jnp.concatenate([m_new] * (2 * tk // LANES), axis=1))
            acc_scr[br, h] = (jnp.exp(m_old - m_new) * acc_scr[br, h]
                              + jnp.dot(p.astype(BF16), vt, preferred_element_type=F32))
            m_scr[br, h] = m_new

    for d in range(nwin + 1):
        pen = None if d == 0 else jnp.where(i >= d, 0.0, NEG)
        mask = "causal" if d == 0 else ("far" if d == nwin else None)
        for h in range(NSA_KVH):
            tile(WIN, h, jnp.maximum(i - d, 0), kw_ref, vw_ref, qw_scr, mask, pen)

    lowest = None
    for h in range(NSA_KVH):
        qb = qbs[h]

        d_c = dc_ref[...] + t0.astype(F32)
        feat_c = jnp.where(lane_m == 0, slope_ref[h] * float(CMP_STRIDE), 0.0).astype(BF16)
        kc_aug = jnp.concatenate([kc_ref[0, 0].astype(BF16), cfeat_ref[...]], axis=1)
        s_c = lax.dot_general(jnp.concatenate([qb, feat_c], axis=1), kc_aug,
                              (((1,), (1,)), ((), ())), preferred_element_type=F32)
        s_c = jnp.where(d_c >= 0.0, s_c, NEG)
        mx = jnp.max(s_c, axis=-1, keepdims=True)
        p = jnp.where(s_c > 0.5 * NEG, jnp.exp(s_c - mx), 0.0)
        p_c = p * (1.0 / jnp.maximum(jnp.sum(p, axis=-1, keepdims=True), 1e-30))
        oc_scr[h] = _dot(p_c, vc_ref[0, 0])

        p_sum = p_c[0:tq]
        for g in range(1, NSA_G):
            p_sum = p_sum + p_c[g * tq:(g + 1) * tq]
        imp = _dot_exact_rhs(p_sum, mmap_ref[...])
        blk = lax.broadcasted_iota(jnp.int32, (tq, ns), 1)
        cur = (t0 + lax.broadcasted_iota(jnp.int32, (tq, ns), 0)) // SLC_BLOCK
        forced = (blk == 0) | (blk == cur) | (blk == cur - 1)
        score = jnp.where(blk <= cur, jnp.where(forced, SEL_FORCE, imp), -SEL_FORCE)
        score = jnp.concatenate([score, jnp.full((tq, LANES - ns), PAD_SCORE, F32)], axis=1)
        ns8 = -(-ns // 8) * 8
        sel_t = _select_top(score.T[0:ns8], ns, n_sel)
        blk_t = lax.broadcasted_iota(jnp.int32, (ns8, tq), 0).astype(F32)
        low = jnp.where(sel_t > 0.5, jnp.where(blk_t >= float(tk // SLC_BLOCK), blk_t, float(ns8)),
                        float(ns8))
        lowest = low if lowest is None else jnp.minimum(lowest, low)
        sel = jnp.concatenate([sel_t, jnp.zeros((LANES - ns8, tq), F32)], axis=0).T
        sel = jnp.concatenate([sel] * NSA_G, axis=0)

        feat_w = qw_scr[h, :, LANES:2 * LANES].astype(F32)
        feat_s = jnp.where(lane_m < ns, jnp.where(sel > 0.5, feat_w, -MASK_BIAS), feat_w)
        qs_scr[h] = jnp.concatenate([qb, feat_s.astype(BF16)], axis=1)

    for h in range(NSA_KVH):
        tile(SEL, h, i, ks_ref, vs_ref, qs_scr, "causal", None)

    lo = jnp.minimum((jnp.min(lowest) * (SLC_BLOCK / tk)).astype(jnp.int32), i)
    n_mid = i - lo
    n_tiles = n_mid + jnp.where(i >= 1, 1, 0)

    def sel_body(jj, carry):
        ia = 2 * jj
        ib = ia + 1
        ja = jnp.where(ia < n_mid, i - 1 - ia, 0)
        jb = jnp.where(ib < n_mid, i - 1 - ib, 0)
        pen_b = jnp.where(ib < n_tiles, 0.0, NEG)
        tile2(SEL, ja, jb, ks_ref, vs_ref, qs_scr, pen_b)
        return carry

    lax.fori_loop(0, (n_tiles + 1) // 2, sel_body, 0)
    o_s = [finish(SEL, h) for h in range(NSA_KVH)]
    o_w = [finish(WIN, h) for h in range(NSA_KVH)]

    gates = g_ref[0]
    out_cols = []
    for h in range(NSA_KVH):
        o_c = oc_scr[h]
        for g in range(NSA_G):
            col = GATE_COL0 + (h * NSA_G + g) * 3
            rs = slice(g * tq, (g + 1) * tq)
            hs = slice(h * NSA_DH, (h + 1) * NSA_DH)
            out_cols.append(gates[:, col:col + 1] * o_c[rs, hs]
                            + gates[:, col + 1:col + 2] * o_s[h][rs, hs]
                            + gates[:, col + 2:col + 3] * o_w[h][rs, hs])
    o_ref[0] = jnp.concatenate(out_cols, axis=1)


def _alibi_slopes():
    h = np.arange(1, NSA_HEADS + 1, dtype=np.float32)
    return np.exp2(-8.0 * h / NSA_HEADS).astype(np.float32).reshape(NSA_KVH, NSA_G)


def _nsa_prompt(qn, kcvc, rows, winr, gts):
    bsz, t, _ = qn.shape
    tq = min(t, 256)
    assert t % tq == 0 and WINDOW % tq == 0 and tq % SLC_BLOCK == 0 and tq % LANES == 0
    nch = kcvc.shape[2]
    ns = -(-t // SLC_BLOCK)
    assert ns < LANES
    n_sel = min(SLC_TOPN, ns)
    m_rows = NSA_G * tq
    slopes = _alibi_slopes()
    qi = np.tile(np.arange(tq, dtype=np.float32), NSA_G)
    slope_rows = np.repeat(slopes, tq, axis=1)
    slope_mat = np.ascontiguousarray(np.broadcast_to(slope_rows[:, :, None], (NSA_KVH, m_rows, LANES)))
    dq_mat = qi[:, None] - np.arange(tq, dtype=np.float32)[None, :]
    assert nch <= 2 * LANES
    cfeat = np.zeros((nch, LANES), np.float32)
    cfeat[:, 0] = np.arange(nch)
    cmp_end = np.arange(nch, dtype=np.float32) * CMP_STRIDE + (CMP_LEN - 1)
    dc_mat = qi[:, None] - cmp_end[None, :]
    mmap = jnp.asarray(_cmp_to_slc_map(nch, ns), BF16)
    key = np.arange(t)
    kfeat = np.zeros((t, LANES), np.float32)
    kfeat[key, key // SLC_BLOCK] = 1.0
    kfeat[:, ns] = key % SLC_BLOCK

    kv = lambda c: pl.BlockSpec((1, t, KV_LANES), lambda b, i: (b, 0, c))
    cst = lambda a: pl.BlockSpec(a.shape, lambda b, i: (0,) * a.ndim)
    consts = [jnp.asarray(slope_mat), jnp.asarray(dq_mat), jnp.asarray(cfeat, BF16),
              jnp.asarray(dc_mat), mmap, jnp.asarray(kfeat, BF16)]
    return pl.pallas_call(
        functools.partial(_nsa_prompt_body, tq=tq, ns=ns, n_sel=n_sel),
        grid=(bsz, t // tq),
        in_specs=[pl.BlockSpec((1, tq, NSA_Q), lambda b, i: (b, i, 0)),
                  pl.BlockSpec((1, 1, nch, KV_LANES), lambda b, i: (0, b, 0, 0)),
                  pl.BlockSpec((1, 1, nch, KV_LANES), lambda b, i: (1, b, 0, 0)),
                  kv(2), kv(3), kv(0), kv(1),
                  pl.BlockSpec((1, tq, LANES), lambda b, i: (b, i, 0))] + [cst(a) for a in consts],
        out_specs=pl.BlockSpec((1, tq, NSA_Q), lambda b, i: (b, i, 0)),
        out_shape=_sds((bsz, t, NSA_Q)),
        scratch_shapes=[pltpu.VMEM((NSA_KVH, m_rows, 2 * LANES), BF16),
                        pltpu.VMEM((NSA_KVH, m_rows, 2 * LANES), BF16),
                        pltpu.VMEM((2, NSA_KVH, m_rows, LANES), F32),
                        pltpu.VMEM((2, NSA_KVH, m_rows, KV_LANES), F32),
                        pltpu.VMEM((NSA_KVH, m_rows, KV_LANES), F32)],
        compiler_params=_params(("arbitrary", "arbitrary")),
        name="nsa_prompt",
    )(qn, kcvc, kcvc, rows, rows, winr, winr, gts, *consts)


def _nsa_sample_body(pt_ref, q_ref, kc_ref, vc_ref, rows_ref, winr_ref, swt_ref, g_ref,
                     slope_ref, pos_ref, cur_ref, dc_ref, gsum_ref, mmap_ref, expand_ref,
                     kfeat_ref, cache_ref, o_ref, wout_ref, kvbuf, sem, *, past, t, npages, ns, n_sel):
    b = pl.program_id(0)
    nb = pl.num_programs(0)
    page = past // npages
    m_rows = NSA_HEADS * t
    half = m_rows // NSA_KVH
    pw = swt_ref.shape[-1]

    def page_copy(bb, slot, p):
        return pltpu.make_async_copy(cache_ref.at[pt_ref[bb, p], pl.ds(2, 2)],
                                     kvbuf.at[slot, :, :, pl.ds(p * page, page)], sem.at[slot])

    @pl.when(b == 0)
    def _():
        for p in range(npages):
            page_copy(0, 0, p).start()

    @pl.when(b + 1 < nb)
    def _():
        for p in range(npages):
            page_copy(b + 1, (b + 1) % 2, p).start()

    pieces = []
    for hh in range(NSA_HEADS):
        qg = q_ref[0, :, hh * NSA_DH:(hh + 1) * NSA_DH]
        z = jnp.zeros_like(qg)
        pieces.append(jnp.concatenate([qg, z] if hh < NSA_G else [z, qg], axis=1))
    qbd = jnp.concatenate(pieces, axis=0).astype(BF16)
    slope = slope_ref[...]
    pos = pos_ref[...]

    def widen(x, n):
        return jnp.concatenate([x] * (n // LANES), axis=1)

    def softmax(parts):
        mx = None
        for s in parts:
            r = jnp.max(s, axis=-1, keepdims=True)
            mx = r if mx is None else jnp.maximum(mx, r)
        ps = [jnp.where(s > 0.5 * NEG, jnp.exp(s - mx), 0.0) for s in parts]
        tot = None
        for p in ps:
            r = jnp.sum(p, axis=-1, keepdims=True)
            tot = r if tot is None else tot + r
        inv = 1.0 / jnp.maximum(tot, 1e-30)
        return [p * inv for p in ps]

    d_c = dc_ref[...]
    nch = d_c.shape[1]
    s_c = _dot_nt(qbd, kc_ref[0, 0])
    (p_c,) = softmax([jnp.where(d_c >= 0.0, s_c - widen(slope, nch) * d_c, NEG)])
    o_c = _dot(p_c, vc_ref[0, 0])

    imp = _dot_exact_lhs(gsum_ref[...], _dot_exact_rhs(p_c, mmap_ref[...]))
    ns8 = imp.shape[1]
    imp = jnp.concatenate([imp, jnp.zeros((m_rows, 2 * LANES - ns8), F32)], axis=1)
    imp = jnp.concatenate([imp, jnp.zeros((LANES - m_rows, 2 * LANES), F32)], axis=0)
    imp_t = imp.T[0:ns8, 0:m_rows]
    blk = lax.broadcasted_iota(jnp.int32, (ns8, m_rows), 0)
    cur = cur_ref[...]
    forced = (blk == 0) | (blk == cur) | (blk == cur - 1)
    score = jnp.where(blk <= cur, jnp.where(forced, SEL_FORCE, imp_t), -SEL_FORCE)
    score = jnp.where(blk < ns, score, PAD_SCORE)
    sel_t = _select_top(score, ns, n_sel)
    sel_t = jnp.concatenate([sel_t, jnp.zeros((ns8, LANES - m_rows), F32)], axis=1)
    sel_t = jnp.concatenate([sel_t, jnp.zeros((2 * LANES - ns8, LANES), F32)], axis=0)
    sel = sel_t.T[0:m_rows, 0:ns8]

    new_ks = rows_ref[0, :, 2 * KV_LANES:3 * KV_LANES]
    new_vs = rows_ref[0, :, 3 * KV_LANES:4 * KV_LANES]
    new_kw = winr_ref[0, :, 0:KV_LANES]
    new_vw = winr_ref[0, :, KV_LANES:2 * KV_LANES]
    dist_n = pos[:, 0:t] - (past + lax.broadcasted_iota(jnp.int32, (m_rows, t), 1)).astype(F32)
    slope_n = slope[:, 0:t]

    kwt = swt_ref[0, 0]
    vwt = swt_ref[0, 1]
    dist_w = widen(pos, pw) - (lax.broadcasted_iota(jnp.int32, (m_rows, pw), 1)
                               + (past - pw)).astype(F32)
    s_w = _dot(qbd, kwt)
    s_w = jnp.where(dist_w >= 0.0,
                    jnp.where(dist_w <= float(WINDOW), s_w - widen(slope, pw) * dist_w, NEG), NEG)
    s_wn = _dot_nt(qbd, new_kw)
    s_wn = jnp.where(dist_n >= 0.0,
                     jnp.where(dist_n <= float(WINDOW), s_wn - slope_n * dist_n, NEG), NEG)
    p_w, p_wn = softmax([s_w, s_wn])
    o_w = _dot_nt(p_w, vwt) + _dot(p_wn, new_vw)
    w_rows = jnp.concatenate([jnp.concatenate([kwt.T, vwt.T], axis=1),
                              winr_ref[0]], axis=0)
    wout_ref[0] = w_rows[pw + t - wout_ref.shape[1]:, :]

    slot = b % 2
    for p in range(npages):
        page_copy(b, slot, p).wait()
    nfull = past // SLC_BLOCK
    lane_f = lax.broadcasted_iota(jnp.int32, (m_rows, 16), 1)
    slope_f = slope[:, 0:16]
    q_feat = jnp.where(lane_f == 0, slope_f * float(SLC_BLOCK),
                       jnp.where(lane_f == 1, slope_f, 0.0)).astype(BF16)
    sel_bias = jnp.where(sel > 0.5, 0.0, -MASK_BIAS).astype(BF16)
    s_s = (_dot(qbd, kvbuf[slot, 0])
           + jnp.dot(sel_bias, expand_ref[...], preferred_element_type=F32)
           + jnp.dot(q_feat, kfeat_ref[...], preferred_element_type=F32))
    row_n = lax.broadcasted_iota(jnp.int32, (m_rows, t), 1).astype(F32)
    s_sn = _dot_nt(qbd, new_ks)
    s_sn = jnp.where(sel[:, nfull:nfull + 1] > 0.5,
                     jnp.where(dist_n >= 0.0, s_sn + slope_n * row_n, NEG), NEG)
    p_s, p_sn = softmax([s_s, s_sn])
    o_s = _dot_nt(p_s, kvbuf[slot, 1]) + _dot(p_sn, new_vs)

    def own(x):
        return jnp.concatenate([x[0:half, 0:NSA_DH], x[half:, NSA_DH:2 * NSA_DH]], axis=0)

    gates = g_ref[0]

    def gate_col(br):
        return jnp.concatenate(
            [gates[:, GATE_COL0 + hh * 3 + br:GATE_COL0 + hh * 3 + br + 1] for hh in range(NSA_HEADS)],
            axis=0)

    mix = gate_col(0) * own(o_c) + gate_col(1) * own(o_s) + gate_col(2) * own(o_w)
    o_ref[0] = jnp.concatenate([mix[hh * t:(hh + 1) * t, :] for hh in range(NSA_HEADS)], axis=1)


def _nsa_sample(qn, kcvc, rows, winr, state_win_t, gts, cache_t, page_table):
    bsz, t, _ = qn.shape
    npages = page_table.shape[1]
    page = cache_t.shape[-1]
    past = npages * page
    pw = state_win_t.shape[-1]
    nch = kcvc.shape[2]
    ns = -(-(past + t) // SLC_BLOCK)
    ns8 = -(-ns // 8) * 8
    m_rows = NSA_HEADS * t
    assert past % SLC_BLOCK == 0 and t <= SLC_BLOCK and t % 8 == 0 and pw + t >= WINDOW
    assert m_rows <= LANES and ns8 <= 2 * LANES and nch % LANES == 0 and pw % LANES == 0
    n_sel = min(SLC_TOPN, ns)
    slopes = _alibi_slopes().reshape(-1)
    lanes1 = np.ones((1, LANES), np.float32)
    slope_b = np.repeat(slopes, t)[:, None].astype(np.float32) * lanes1
    pos_i = np.tile(past + np.arange(t), NSA_HEADS)
    cmp_end = np.arange(nch, dtype=np.float32) * CMP_STRIDE + (CMP_LEN - 1)
    dc_mat = pos_i[:, None].astype(np.float32) - cmp_end[None, :]
    row_kvh = np.arange(m_rows) // (NSA_G * t)
    row_tok = np.arange(m_rows) % t
    gsum = ((row_kvh[:, None] == row_kvh[None, :]) & (row_tok[:, None] == row_tok[None, :]))
    mmap = np.zeros((nch, ns8), np.float32)
    mmap[:, :ns] = _cmp_to_slc_map(nch, ns)
    expand = (np.arange(ns8)[:, None] == (np.arange(past) // SLC_BLOCK)[None, :])
    assert past // SLC_BLOCK <= 2 * LANES
    kfeat = np.zeros((16, past), np.float32)
    kfeat[0] = np.arange(past) // SLC_BLOCK - past // SLC_BLOCK
    kfeat[1] = np.arange(past) % SLC_BLOCK
    consts = [jnp.asarray(slope_b), jnp.asarray(pos_i[:, None].astype(np.float32) * lanes1),
              jnp.asarray((pos_i // SLC_BLOCK)[None, :].astype(np.int32)), jnp.asarray(dc_mat),
              jnp.asarray(gsum.astype(np.float32), BF16), jnp.asarray(mmap, BF16),
              jnp.asarray(expand.astype(np.float32), BF16), jnp.asarray(kfeat, BF16)]

    cst = lambda a: pl.BlockSpec(a.shape, lambda b, pt: (0,) * a.ndim)
    grid_spec = pltpu.PrefetchScalarGridSpec(
        num_scalar_prefetch=1,
        grid=(bsz,),
        in_specs=[pl.BlockSpec((1, t, NSA_Q), lambda b, pt: (b, 0, 0)),
                  pl.BlockSpec((1, 1, nch, KV_LANES), lambda b, pt: (0, b, 0, 0)),
                  pl.BlockSpec((1, 1, nch, KV_LANES), lambda b, pt: (1, b, 0, 0)),
                  pl.BlockSpec((1, t, ROWS_LANES), lambda b, pt: (b, 0, 0)),
                  pl.BlockSpec((1, t, 2 * KV_LANES), lambda b, pt: (b, 0, 0)),
                  pl.BlockSpec((1, 2, KV_LANES, pw), lambda b, pt: (b, 0, 0, 0)),
                  pl.BlockSpec((1, t, LANES), lambda b, pt: (b, 0, 0))]
        + [cst(a) for a in consts] + [pl.BlockSpec(memory_space=pl.ANY)],
        out_specs=[pl.BlockSpec((1, t, NSA_Q), lambda b, pt: (b, 0, 0)),
                   pl.BlockSpec((1, WINDOW, 2 * KV_LANES), lambda b, pt: (b, 0, 0))],
        scratch_shapes=[pltpu.VMEM((2, 2, KV_LANES, past), F32), pltpu.SemaphoreType.DMA((2,))],
    )
    return pl.pallas_call(
        functools.partial(_nsa_sample_body, past=past, t=t, npages=npages, ns=ns, n_sel=n_sel),
        grid_spec=grid_spec,
        out_shape=[_sds((bsz, t, NSA_Q)), _sds((bsz, WINDOW, 2 * KV_LANES))],
        compiler_params=_params(("arbitrary",)),
        name="nsa_sample",
    )(page_table, qn, kcvc, kcvc, rows, winr, state_win_t, gts, *consts, cache_t)


def _ffn_body(x_ref, og_ref, on_ref, gtm_ref, scf_ref, shf_ref, gtf_ref,
              nmp_ref, nfp_ref, nfo_ref, wo_ref, wu_ref, wd_ref, y_ref, *, ff_chunk):
    mix = (jnp.dot(og_ref[0].astype(BF16), wo_ref[0:GLA_V, :], preferred_element_type=F32)
           + jnp.dot(on_ref[0].astype(BF16), wo_ref[GLA_V:GLA_V + NSA_Q, :],
                     preferred_element_type=F32))
    x1 = x_ref[0] + gtm_ref[0] * _rms(mix, nmp_ref[...])
    hb = (_rms(x1, nfp_ref[...]) * (1.0 + scf_ref[0]) + shf_ref[0]).astype(BF16)
    f = None
    for c in range(wu_ref.shape[1] // ff_chunk):
        cs = slice(c * ff_chunk, (c + 1) * ff_chunk)
        u = jnp.maximum(jnp.dot(hb, wu_ref[:, cs], preferred_element_type=F32), 0.0)
        term = jnp.dot((u * u).astype(BF16), wd_ref[cs, :], preferred_element_type=F32)
        f = term if f is None else f + term
    y_ref[0] = x1 + gtf_ref[0] * _rms(f, nfo_ref[...])


def _ffn(x, og, on, gtm, scf, shf, gtf, nmp, nfp, nfo, wo, wu, wd):
    bsz, t, d = x.shape
    tq = min(t, 512)
    assert t % tq == 0
    tok = lambda n: pl.BlockSpec((1, tq, n), lambda b, i: (b, i, 0))
    full = lambda a: pl.BlockSpec(a.shape, lambda b, i: (0,) * a.ndim,
                                  pipeline_mode=pl.Buffered(1))
    vec = lambda a: pl.BlockSpec(a.shape, lambda b, i: (0,) * a.ndim)
    return pl.pallas_call(
        functools.partial(_ffn_body, ff_chunk=min(wu.shape[1], 1024)),
        grid=(bsz, t // tq),
        in_specs=[tok(d), tok(GLA_V), tok(NSA_Q), _mod_spec(gtm, tq), _mod_spec(scf, tq),
                  _mod_spec(shf, tq), _mod_spec(gtf, tq), vec(nmp), vec(nfp), vec(nfo),
                  full(wo), full(wu), full(wd)],
        out_specs=tok(d),
        out_shape=_sds((bsz, t, d)),
        compiler_params=_params(("arbitrary", "arbitrary")),
        name="ffn",
    )(x, og, on, gtm, scf, shf, gtf, nmp, nfp, nfo, wo, wu, wd)


def _permute_w_in(w_in):
    o_zq = 0
    o_zk = o_zq + GLA_QK
    o_zv = o_zk + GLA_QK
    o_za = o_zv + GLA_V
    o_zr = o_za + GLA_RANK
    o_zqn = o_zr + GLA_V
    o_zkv = o_zqn + NSA_Q
    o_zg = o_zkv + NSA_KV
    o_end = o_zg + NSA_GATE
    d = w_in.shape[0]
    pad = jnp.zeros((d, LANES - GLA_RANK - NSA_GATE), w_in.dtype)
    cols = [w_in[:, o_zq:o_za], w_in[:, o_zr:o_zg], w_in[:, o_za:o_zr], w_in[:, o_zg:o_end], pad]
    return jnp.concatenate(cols, axis=1).astype(BF16)


def _layer_weights(l, norm_mix_pre, norm_mix_post, norm_ffn_pre, norm_ffn_post, w_in,
                   gla_w_gate, gla_b_gate, gla_norm, cmp_pos, cmp_w1, cmp_b1, cmp_w2, cmp_b2,
                   w_out, w_up, w_down):
    wg_pad = jnp.zeros((LANES, GLA_QK), F32).at[:GLA_RANK].set(gla_w_gate[l]).astype(BF16)
    w1, w2 = _compress_weights(cmp_w1[l], cmp_w2[l])
    return dict(
        nmpre=norm_mix_pre[l][None, :], nmpost=norm_mix_post[l][None, :],
        nfpre=norm_ffn_pre[l][None, :], nfpost=norm_ffn_post[l][None, :],
        w_perm=_permute_w_in(w_in[l]), wg_pad=wg_pad, bg=gla_b_gate[l][None, :],
        gnorm=gla_norm[l][None, :], w1=w1, w2=w2, posb=_posbias(cmp_pos[l], cmp_w1[l]),
        b1=cmp_b1[l][:, None, :], b2=cmp_b2[l][:, None, :],
        wo=w_out[l].astype(BF16), wu=w_up[l].astype(BF16), wd=w_down[l].astype(BF16))


def _split_ada(ada, rows_per_batch):
    parts = jnp.split(ada, 6, axis=-1)
    if rows_per_batch is None:
        return [p[:, None, :] for p in parts]
    return [jnp.repeat(p, rows_per_batch, axis=0)[None] for p in parts]


def _prompt_layer(x, ada, w, wbuf):
    bsz, t, d = x.shape
    sh_m, sc_m, gt_m, sh_f, sc_f, gt_f = _split_ada(ada, None)
    qg, kg, vg, rg, qn, rows, winr, logf, gts = _inproj(x, sc_m, sh_m, w["nmpre"], w["w_perm"],
                                                        w["wg_pad"], w["bg"])
    s0 = jnp.zeros((bsz, GLA_HEADS, GLA_DK, GLA_DV), F32)
    og, s_fin = _gla(qg, kg, vg, logf, rg, s0, w["gnorm"])
    kcvc = _compress_prompt(rows, w["w1"], w["posb"], w["b1"], w["w2"], w["b2"])
    on = _nsa_prompt(qn, kcvc, rows, winr, gts)
    y = _ffn(x, og, on, gt_m, sc_f, sh_f, gt_f, w["nmpost"], w["nfpre"], w["nfpost"],
             w["wo"], w["wu"], w["wd"])
    win = winr.reshape(bsz, t, 2, NSA_KVH, NSA_DH)
    if t < wbuf:
        win = jnp.pad(win, ((0, 0), (wbuf - t, 0), (0, 0), (0, 0), (0, 0)))
    return y, rows.reshape(bsz, t, 4, NSA_KVH, NSA_DH), win[:, -wbuf:], s_fin


def _sample_layer(x, ada, cache, page_table, state_win, state_gla, w):
    bsz, t, d = x.shape
    n = bsz * t
    sh_m, sc_m, gt_m, sh_f, sc_f, gt_f = _split_ada(ada, t)
    outs = _inproj(x.reshape(1, n, d), sc_m, sh_m, w["nmpre"], w["w_perm"], w["wg_pad"], w["bg"])
    qg, kg, vg, rg, qn, rows, winr, logf, gts = [o.reshape(bsz, t, o.shape[-1]) for o in outs]
    og, s_fin = _gla(qg, kg, vg, logf, rg, state_gla, w["gnorm"])
    n_pool, page = cache.shape[0], cache.shape[1]
    cache_t = jnp.transpose(cache, (0, 2, 3, 4, 1)).reshape(n_pool, 4, KV_LANES, page)
    pw = state_win.shape[1]
    state_win_t = jnp.transpose(state_win, (0, 2, 3, 4, 1)).reshape(bsz, 2, KV_LANES, pw)
    kcvc = _compress_paged(cache_t, page_table, w["w1"], w["posb"], w["b1"], w["w2"], w["b2"])
    on, win = _nsa_sample(qn, kcvc, rows, winr, state_win_t, gts, cache_t, page_table)
    y = _ffn(x.reshape(1, n, d), og.reshape(1, n, GLA_V), on.reshape(1, n, NSA_Q),
             gt_m, sc_f, sh_f, gt_f, w["nmpost"], w["nfpre"], w["nfpost"],
             w["wo"], w["wu"], w["wd"])
    return (y.reshape(bsz, t, d), rows.reshape(bsz, t, 4, NSA_KVH, NSA_DH),
            win.reshape(bsz, WINDOW, 2, NSA_KVH, NSA_DH), s_fin)


def kernel(x_prompt, x_sample, cache_kv, state_win, state_gla, page_table, c_prompt, c_sample,
           norm_mix_pre, norm_mix_post, norm_ffn_pre, norm_ffn_post, w_ada, b_ada, w_in,
           gla_w_gate, gla_b_gate, gla_norm, cmp_pos, cmp_w1, cmp_b1, cmp_w2, cmp_b2,
           w_out, w_up, w_down):
    depth = w_in.shape[0]
    bsz = x_prompt.shape[0]
    wbuf = state_win.shape[2]
    assert wbuf == WINDOW
    c_all = jnp.concatenate([c_prompt, c_sample], axis=0)
    y_p, y_s = x_prompt, x_sample
    outs = [[] for _ in range(6)]
    for l in range(depth):
        w = _layer_weights(l, norm_mix_pre, norm_mix_post, norm_ffn_pre, norm_ffn_post, w_in,
                           gla_w_gate, gla_b_gate, gla_norm, cmp_pos, cmp_w1, cmp_b1, cmp_w2,
                           cmp_b2, w_out, w_up, w_down)
        ada = _ada(c_all, w_ada[l], b_ada[l])
        y_p, r_p, w_p, s_p = _prompt_layer(y_p, ada[:bsz], w, wbuf)
        y_s, r_s, w_s, s_s = _sample_layer(y_s, ada[bsz:], cache_kv[l], page_table,
                                           state_win[l], state_gla[l], w)
        for lst, val in zip(outs, (r_p, r_s, w_p, w_s, s_p, s_s)):
            lst.append(val)
    return (y_p, y_s) + tuple(jnp.stack(o) for o in outs)
```

```python
import functools
import math

import numpy as np
import jax
import jax.numpy as jnp
from jax import lax
from jax.experimental import pallas as pl
from jax.experimental.pallas import tpu as pltpu

F32 = jnp.float32
BF16 = jnp.bfloat16

GLA_HEADS = 4
GLA_DK = 64
GLA_DV = 128
GLA_RANK = 16
GLA_NORMALIZER = 16.0
GLA_CHUNK = 64
NSA_HEADS = 8
NSA_DH = 64
NSA_KVH = 2
NSA_G = NSA_HEADS // NSA_KVH
CMP_STRIDE = 16
CMP_LEN = 32
CMP_HIDDEN = 256
SLC_BLOCK = 64
SLC_TOPN = 16
SEL_FORCE = 1.0e4
WINDOW = 512
EPS = 1e-6
NEG = -1.0e30
PAD_SCORE = -3.0e4
MASK_BIAS = 2.0 ** 60
LANES = 128
VMEM_LIMIT = 56 * 1024 * 1024

GLA_QK = GLA_HEADS * GLA_DK
GLA_V = GLA_HEADS * GLA_DV
NSA_Q = NSA_HEADS * NSA_DH
NSA_KV = 6 * NSA_KVH * NSA_DH
NSA_GATE = 3 * NSA_HEADS
KV_LANES = NSA_KVH * NSA_DH
ROWS_LANES = 4 * KV_LANES


def _sds(shape, dtype=F32):
    return jax.ShapeDtypeStruct(shape, dtype)


def _params(sem):
    return pltpu.CompilerParams(dimension_semantics=sem, vmem_limit_bytes=VMEM_LIMIT)


def _dot(a, b):
    return jnp.dot(a.astype(BF16), b.astype(BF16), preferred_element_type=F32)


def _dot_nt(a, b):
    return lax.dot_general(a.astype(BF16), b.astype(BF16), (((1,), (1,)), ((), ())),
                           preferred_element_type=F32)


def _dot_tn(a, b):
    return lax.dot_general(a.astype(BF16), b.astype(BF16), (((0,), (0,)), ((), ())),
                           preferred_element_type=F32)


def _split3(x):
    p1 = x.astype(BF16)
    r1 = x - p1.astype(F32)
    p2 = r1.astype(BF16)
    p3 = (r1 - p2.astype(F32)).astype(BF16)
    return p1, p2, p3


def _dot_f32(a, b):
    a1, a2, _ = _split3(a)
    b1, b2, _ = _split3(b)
    return (jnp.dot(a1, b1, preferred_element_type=F32)
            + jnp.dot(a1, b2, preferred_element_type=F32)
            + jnp.dot(a2, b1, preferred_element_type=F32))


def _dot_exact_lhs(a_bf16, x):
    out = None
    for p in _split3(x):
        t = jnp.dot(a_bf16, p, preferred_element_type=F32)
        out = t if out is None else out + t
    return out


def _dot_exact_rhs(x, b_bf16):
    out = None
    for p in _split3(x):
        t = jnp.dot(p, b_bf16, preferred_element_type=F32)
        out = t if out is None else out + t
    return out


def _sigmoid(x):
    return 1.0 / (1.0 + jnp.exp(-x))


def _silu(x):
    return x * _sigmoid(x)


def _rms(x, g):
    return x * lax.rsqrt(jnp.mean(x * x, axis=-1, keepdims=True) + EPS) * g


def _ada_body(c_ref, w_ref, b_ref, o_ref):
    o_ref[...] = _dot_f32(_silu(c_ref[...]), w_ref[...]) + b_ref[...]


def _ada(c, w_ada, b_ada):
    n, d = c.shape
    nout = w_ada.shape[1]
    tn = d
    return pl.pallas_call(
        _ada_body,
        grid=(nout // tn,),
        in_specs=[pl.BlockSpec((n, d), lambda j: (0, 0)),
                  pl.BlockSpec((d, tn), lambda j: (0, j)),
                  pl.BlockSpec((1, tn), lambda j: (0, j))],
        out_specs=pl.BlockSpec((n, tn), lambda j: (0, j)),
        out_shape=_sds((n, nout)),
        compiler_params=_params(("arbitrary",)),
        name="ada",
    )(c, w_ada, b_ada.reshape(1, nout))


_C_ZQ = 0
_C_ZK = _C_ZQ + GLA_QK
_C_ZV = _C_ZK + GLA_QK
_C_ZR = _C_ZV + GLA_V
_C_ZQN = _C_ZR + GLA_V
_C_ZKV = _C_ZQN + NSA_Q
_C_MISC = _C_ZKV + NSA_KV
_C_END = _C_MISC + LANES
GATE_COL0 = GLA_RANK


def _inproj_body(x_ref, sc_ref, sh_ref, g_ref, w_ref, wg_ref, bg_ref,
                 qg_ref, kg_ref, vg_ref, rg_ref, qn_ref, rows_ref, winr_ref, logf_ref, gts_ref):
    h = _rms(x_ref[0], g_ref[...]) * (1.0 + sc_ref[0]) + sh_ref[0]
    hb = h.astype(BF16)

    def proj(a, b):
        return jnp.dot(hb, w_ref[:, a:b], preferred_element_type=F32)

    qg_ref[0] = proj(_C_ZQ, _C_ZK) * (GLA_DK ** -0.5)
    kg_ref[0] = proj(_C_ZK, _C_ZV)
    vg_ref[0] = proj(_C_ZV, _C_ZR)
    rg_ref[0] = _silu(proj(_C_ZR, _C_ZQN))
    qn_ref[0] = proj(_C_ZQN, _C_ZKV) * (NSA_DH ** -0.5)
    rows_ref[0] = proj(_C_ZKV, _C_ZKV + ROWS_LANES)
    winr_ref[0] = proj(_C_ZKV + ROWS_LANES, _C_MISC)
    zm = proj(_C_MISC, _C_END)
    gts_ref[0] = _sigmoid(zm)
    logit = jnp.dot(zm.astype(BF16), wg_ref[...], preferred_element_type=F32) + bg_ref[...]
    log_sig = jnp.minimum(logit, 0.0) - jnp.log(1.0 + jnp.exp(-jnp.abs(logit)))
    logf_ref[0] = log_sig * (1.0 / GLA_NORMALIZER)


def _mod_spec(mod, tq):
    d = mod.shape[-1]
    if mod.shape[1] == 1:
        return pl.BlockSpec((1, 1, d), lambda b, t: (b, 0, 0))
    return pl.BlockSpec((1, tq, d), lambda b, t: (b, t, 0))


def _inproj(x, sc, sh, gain, w_perm, wg_pad, bg):
    bsz, t, d = x.shape
    tq = min(t, 512)
    assert t % tq == 0
    widths = (GLA_QK, GLA_QK, GLA_V, GLA_V, NSA_Q, ROWS_LANES, NSA_KV - ROWS_LANES, GLA_QK, LANES)
    tok = lambda n: pl.BlockSpec((1, tq, n), lambda b, i: (b, i, 0))
    full = lambda a: pl.BlockSpec(a.shape, lambda b, i: (0,) * a.ndim)
    return pl.pallas_call(
        _inproj_body,
        grid=(bsz, t // tq),
        in_specs=[tok(d), _mod_spec(sc, tq), _mod_spec(sh, tq), full(gain), full(w_perm),
                  full(wg_pad), full(bg)],
        out_specs=[tok(n) for n in widths],
        out_shape=[_sds((bsz, t, n)) for n in widths],
        compiler_params=_params(("arbitrary", "arbitrary")),
        name="inproj",
    )(x, sc, sh, gain, w_perm, wg_pad, bg)


def _head_stack(x, head_of_lane):
    return jnp.concatenate([jnp.where(head_of_lane == h, x, 0.0) for h in range(GLA_HEADS)],
                           axis=0)


def _gla_body(q_ref, k_ref, v_ref, f_ref, r_ref, s0_ref, gn_ref, o_ref, sfin_ref, s_scr,
              *, rows, nchunks, nb):
    c_len = GLA_CHUNK
    t = pl.program_id(1)

    @pl.when(t == 0)
    def _():
        s_scr[...] = s0_ref[...]

    n = nchunks * c_len

    def load(ref, bb):
        x = ref[bb]
        if rows < n:
            x = jnp.concatenate([x, jnp.zeros((n - rows, x.shape[1]), F32)], axis=0)
        return x

    ri = lax.broadcasted_iota(jnp.int32, (n, n), 0)
    ci = lax.broadcasted_iota(jnp.int32, (n, n), 1)
    shift = c_len.bit_length() - 1
    same_chunk = lax.shift_right_logical(ri, shift) == lax.shift_right_logical(ci, shift)
    tril = jnp.where(ci <= ri, jnp.where(same_chunk, 1.0, 0.0), 0.0).astype(BF16)
    causal = (lax.broadcasted_iota(jnp.int32, (c_len, c_len), 1)
              <= lax.broadcasted_iota(jnp.int32, (c_len, c_len), 0))
    head_of_lane = lax.broadcasted_iota(jnp.int32, (c_len, GLA_QK), 1) // GLA_DK

    seqs = []
    for bb in range(nb):
        q, k, v, g, r = (load(ref, bb) for ref in (q_ref, k_ref, v_ref, f_ref, r_ref))
        b = None
        for gp in _split3(g):
            term = jnp.dot(tril, gp, preferred_element_type=F32)
            b = term if b is None else b + term
        seqs.append((q, k, v, r, b, [s_scr[bb, h] for h in range(GLA_HEADS)]))

    for c in range(nchunks):
        cs = slice(c * c_len, (c + 1) * c_len)
        lo = c * c_len
        for bb, (q, k, v, r, b, state) in enumerate(seqs):
            b_c = b[cs]
            b_last = b_c[c_len - 1:c_len, :]
            decay_t = jnp.exp(jnp.broadcast_to(b_last, (GLA_DV, GLA_QK)).T)
            b_mid = b_c[c_len // 2 - 1:c_len // 2, :]
            q_inter = q[cs] * jnp.exp(b_c)
            q_intra = q[cs] * jnp.exp(b_c - b_mid)
            k_intra = k[cs] * jnp.exp(b_mid - b_c)
            k_out = k[cs] * jnp.exp(b_last - b_c)
            v_c = v[cs]
            k_out_t = k_out.T
            a_all = _dot_nt(_head_stack(q_intra, head_of_lane), k_intra)
            o_all = _dot(_head_stack(q_inter, head_of_lane), jnp.concatenate(state, axis=0))
            outs = []
            for h in range(GLA_HEADS):
                ks = slice(h * GLA_DK, (h + 1) * GLA_DK)
                vs = slice(h * GLA_DV, (h + 1) * GLA_DV)
                hs = slice(h * c_len, (h + 1) * c_len)
                a = jnp.where(causal, a_all[hs], 0.0)
                av = _dot(jnp.concatenate([a, k_out_t[ks, :]], axis=0), v_c[:, vs])
                o = av[0:c_len] + o_all[hs]
                state[h] = state[h] * decay_t[ks, :] + av[c_len:]
                outs.append(_rms(o, gn_ref[...]))
            on = jnp.concatenate(outs, axis=1) * r[cs]
            o_ref[bb, lo:min(lo + c_len, rows), :] = on[:min(c_len, rows - lo)]
    for bb, seq in enumerate(seqs):
        for h in range(GLA_HEADS):
            s_scr[bb, h] = seq[5][h]

    @pl.when(t == pl.num_programs(1) - 1)
    def _():
        sfin_ref[...] = s_scr[...]


def _gla(qg, kg, vg, logf, rg, s0, gnorm):
    bsz, t, _ = qg.shape
    tt = min(t, 4 * GLA_CHUNK)
    assert t % tt == 0 and (tt % GLA_CHUNK == 0 or tt < GLA_CHUNK)
    nchunks = max(tt // GLA_CHUNK, 1)
    nb = math.gcd(bsz, 8 if nchunks > 1 else 16)
    tok = lambda n: pl.BlockSpec((nb, tt, n), lambda b, i: (b, i, 0))
    st = pl.BlockSpec((nb, GLA_HEADS, GLA_DK, GLA_DV), lambda b, i: (b, 0, 0, 0))
    return pl.pallas_call(
        functools.partial(_gla_body, rows=tt, nchunks=nchunks, nb=nb),
        grid=(bsz // nb, t // tt),
        in_specs=[tok(GLA_QK), tok(GLA_QK), tok(GLA_V), tok(GLA_QK), tok(GLA_V), st,
                  pl.BlockSpec((1, GLA_DV), lambda b, i: (0, 0))],
        out_specs=[tok(GLA_V), st],
        out_shape=[_sds((bsz, t, GLA_V)), _sds((bsz, GLA_HEADS, GLA_DK, GLA_DV))],
        scratch_shapes=[pltpu.VMEM((nb, GLA_HEADS, GLA_DK, GLA_DV), F32)],
        compiler_params=_params(("arbitrary", "arbitrary")),
        name="gla",
    )(qg, kg, vg, logf, rg, s0, gnorm)


def _posbias_body(p_ref, w_ref, o_ref):
    o_ref[0, 0] = _dot_f32(p_ref[0, 0], w_ref[0, 0])


def _posbias(cmp_pos, cmp_w1):
    kdim = CMP_STRIDE * NSA_DH
    pos = jnp.broadcast_to(cmp_pos.reshape(2, 2, 1, kdim), (2, 2, 8, kdim))
    w = cmp_w1.reshape(2, 2, kdim, CMP_HIDDEN)
    return pl.pallas_call(
        _posbias_body,
        grid=(2, 2),
        in_specs=[pl.BlockSpec((1, 1, 8, kdim), lambda a, b: (a, b, 0, 0)),
                  pl.BlockSpec((1, 1, kdim, CMP_HIDDEN), lambda a, b: (a, b, 0, 0))],
        out_specs=pl.BlockSpec((1, 1, 8, CMP_HIDDEN), lambda a, b: (a, b, 0, 0)),
        out_shape=_sds((2, 2, 8, CMP_HIDDEN)),
        compiler_params=_params(("arbitrary", "arbitrary")),
        name="posbias",
    )(pos, w)


def _compress_rows(x_ref, w1_ref, pb_ref, b1_ref, w2_ref, b2_ref):
    nch = x_ref.shape[0] // CMP_STRIDE
    lo = lax.broadcasted_iota(jnp.int32, (nch, KV_LANES), 1) < NSA_DH
    cols = [[], []]
    for jp in range(CMP_STRIDE // 2):
        xe = x_ref[pl.ds(2 * jp, nch, stride=CMP_STRIDE), :]
        xo = x_ref[pl.ds(2 * jp + 1, nch, stride=CMP_STRIDE), :]
        cols[0].append(jnp.where(lo, xe, pltpu.roll(xo, NSA_DH, axis=1)).astype(BF16))
        cols[1].append(jnp.where(lo, pltpu.roll(xe, NSA_DH, axis=1), xo).astype(BF16))
    bias = pb_ref[0, 0, 0:1, :] + pb_ref[0, 1, 0:1, :] + b1_ref[0]
    x_all = jnp.concatenate([jnp.concatenate(cols[h], axis=1) for h in range(NSA_KVH)], axis=0)
    z_all = jnp.dot(x_all, w1_ref[0], preferred_element_type=F32)
    hid = []
    for h in range(NSA_KVH):
        z = z_all[h * nch:(h + 1) * nch]
        z_second = z[:, CMP_HIDDEN:2 * CMP_HIDDEN]
        z_next = jnp.concatenate([z_second[1:], jnp.zeros((1, CMP_HIDDEN), F32)], axis=0)
        hid.append(_silu(z[:, 0:CMP_HIDDEN] + z_next + bias))
    out_all = _dot(jnp.concatenate(hid, axis=0), w2_ref[0]) + b2_ref[0]
    outs = [out_all[h * nch:(h + 1) * nch] for h in range(NSA_KVH)]
    return jnp.concatenate(outs, axis=1)


def _compress_weights(cmp_w1, cmp_w2):
    kdim = CMP_STRIDE * NSA_DH
    w1 = jnp.concatenate([cmp_w1[:, :CMP_STRIDE].reshape(2, kdim, CMP_HIDDEN),
                          cmp_w1[:, CMP_STRIDE:].reshape(2, kdim, CMP_HIDDEN)], axis=-1)
    return w1.astype(BF16), cmp_w2.astype(BF16)


def _compress_wspec(shape):
    nd = len(shape)
    return pl.BlockSpec((1,) + shape[1:], lambda b, cc, *_: (cc,) + (0,) * (nd - 1))


def _compress_prompt_body(x_ref, w1_ref, pb_ref, b1_ref, w2_ref, b2_ref, o_ref):
    o_ref[0, 0] = _compress_rows(x_ref.at[0], w1_ref, pb_ref, b1_ref, w2_ref, b2_ref)


def _compress_prompt(rows, w1, posb, b1, w2, b2):
    bsz, t, _ = rows.shape
    assert t % LANES == 0
    nch = t // CMP_STRIDE
    weights = (w1, posb, b1, w2, b2)
    return pl.pallas_call(
        _compress_prompt_body,
        grid=(bsz, 2),
        in_specs=[pl.BlockSpec((1, t, KV_LANES), lambda b, cc: (b, 0, cc))]
        + [_compress_wspec(a.shape) for a in weights],
        out_specs=pl.BlockSpec((1, 1, nch, KV_LANES), lambda b, cc: (cc, b, 0, 0)),
        out_shape=_sds((2, bsz, nch, KV_LANES)),
        compiler_params=_params(("arbitrary", "arbitrary")),
        name="compress_prompt",
    )(rows, *weights)


def _compress_paged_body(pt_ref, cache_ref, w1_ref, pb_ref, b1_ref, w2_ref, b2_ref, o_ref,
                         xt_buf, x_scr, sem, *, npages):
    b = pl.program_id(0)
    cc = pl.program_id(1)
    step = b * 2 + cc
    nsteps = pl.num_programs(0) * 2
    page = xt_buf.shape[-1]

    def page_copy(bb, c, slot, p):
        return pltpu.make_async_copy(cache_ref.at[pt_ref[bb, p], c], xt_buf.at[slot, p],
                                     sem.at[slot])

    @pl.when(step == 0)
    def _():
        for p in range(npages):
            page_copy(0, 0, 0, p).start()

    @pl.when(step + 1 < nsteps)
    def _():
        nxt = step + 1
        for p in range(npages):
            page_copy(nxt // 2, nxt % 2, nxt % 2, p).start()

    slot = step % 2
    for p in range(npages):
        page_copy(b, cc, slot, p).wait()
    for p in range(npages):
        x_scr[p * page:(p + 1) * page, :] = xt_buf[slot, p].T
    o_ref[0, 0] = _compress_rows(x_scr, w1_ref, pb_ref, b1_ref, w2_ref, b2_ref)


def _compress_paged(cache_t, page_table, w1, posb, b1, w2, b2):
    bsz, npages = page_table.shape
    page = cache_t.shape[-1]
    assert page == LANES and cache_t.shape[2] == KV_LANES
    nch = npages * page // CMP_STRIDE
    weights = (w1, posb, b1, w2, b2)
    grid_spec = pltpu.PrefetchScalarGridSpec(
        num_scalar_prefetch=1,
        grid=(bsz, 2),
        in_specs=[pl.BlockSpec(memory_space=pl.ANY)] + [_compress_wspec(a.shape) for a in weights],
        out_specs=pl.BlockSpec((1, 1, nch, KV_LANES), lambda b, cc, pt: (cc, b, 0, 0)),
        scratch_shapes=[pltpu.VMEM((2, npages, KV_LANES, page), F32),
                        pltpu.VMEM((npages * page, KV_LANES), F32),
                        pltpu.SemaphoreType.DMA((2,))],
    )
    return pl.pallas_call(
        functools.partial(_compress_paged_body, npages=npages),
        grid_spec=grid_spec,
        out_shape=_sds((2, bsz, nch, KV_LANES)),
        compiler_params=_params(("arbitrary", "arbitrary")),
        name="compress_paged",
    )(page_table, cache_t, *weights)


def _cmp_to_slc_map(nc, ns):
    start = np.arange(nc) * CMP_STRIDE
    bs = np.arange(ns) * SLC_BLOCK
    ov = (np.minimum(start[:, None] + CMP_LEN, bs[None, :] + SLC_BLOCK)
          - np.maximum(start[:, None], bs[None, :]))
    return (np.clip(ov, 0, None) / CMP_LEN).astype(np.float32)


def _select_top(score_t, ns, n_sel):
    nrow, nq = score_t.shape
    groups = [score_t[8 * v:8 * v + 8, :] for v in range(nrow // 8)]
    ranks = [jnp.zeros((8, nq), F32) for _ in groups]
    sub = lax.broadcasted_iota(jnp.int32, (8, nq), 0)
    for i in range(ns):
        row = jnp.broadcast_to(score_t[i:i + 1, :], (8, nq))
        for v, grp in enumerate(groups):
            if i < 8 * v:
                ranks[v] = jnp.where(row >= grp, ranks[v] + 1.0, ranks[v])
            elif i >= 8 * v + 8:
                ranks[v] = jnp.where(row > grp, ranks[v] + 1.0, ranks[v])
            else:
                later = jnp.where(row >= grp, 1.0, 0.0)
                earlier = jnp.where(row > grp, 1.0, 0.0)
                ranks[v] = ranks[v] + jnp.where(sub > (i - 8 * v), later, earlier)
    rank = jnp.concatenate(ranks, axis=0)
    return jnp.where(rank < float(n_sel), 1.0, 0.0)


def _nsa_prompt_body(q_ref, kc_ref, vc_ref, ks_ref, vs_ref, kw_ref, vw_ref, g_ref,
                     slope_ref, dq_ref, cfeat_ref, dc_ref, mmap_ref, kfeat_ref,
                     o_ref, qs_scr, qw_scr, m_scr, acc_scr, oc_scr, *, tq, ns, n_sel):
    i = pl.program_id(1)
    t0 = i * tq
    tk = tq
    m_rows = NSA_G * tq
    nwin = WINDOW // tk
    lane_q = lax.broadcasted_iota(jnp.int32, (tq, KV_LANES), 1)
    lane_m = lax.broadcasted_iota(jnp.int32, (m_rows, LANES), 1)
    lane_v = lax.broadcasted_iota(jnp.int32, (tk, KV_LANES), 1)
    dq = dq_ref[...]
    rel_blk = ((lane_m - t0 // SLC_BLOCK) * SLC_BLOCK).astype(F32)
    SEL, WIN = 0, 1

    def tile(br, h, j, k_ref, v_ref, q_scr, mask, pen):
        k0 = pl.multiple_of(j * tk, tk)
        kt = jnp.concatenate([k_ref[0, pl.ds(k0, tk), :].astype(BF16),
                              kfeat_ref[pl.ds(k0, tk), :]], axis=1)
        s = lax.dot_general(q_scr[h], kt, (((1,), (1,)), ((), ())), preferred_element_type=F32)
        if mask == "causal":
            s = jnp.where(dq >= 0.0, s, NEG)
        elif mask == "far":
            s = jnp.where(dq <= 0.0, s, NEG)
        if pen is not None:
            s = s + pen
        vt = jnp.where((lane_v // NSA_DH) == h, v_ref[0, pl.ds(k0, tk), :], 1.0).astype(BF16)
        if mask == "causal":
            m_new = jnp.broadcast_to(jnp.max(s, axis=-1, keepdims=True), (m_rows, LANES))
            p = jnp.exp(s - jnp.concatenate([m_new] * (tk // LANES), axis=1))
            acc_scr[br, h] = jnp.dot(p.astype(BF16), vt, preferred_element_type=F32)
        else:
            m_old = m_scr[br, h]
            m_new = jnp.maximum(m_old, jnp.max(s, axis=-1, keepdims=True))
            p = jnp.exp(s - jnp.concatenate([m_new] * (tk // LANES), axis=1))
            acc_scr[br, h] = (jnp.exp(m_old - m_new) * acc_scr[br, h]
                              + jnp.dot(p.astype(BF16), vt, preferred_element_type=F32))
        m_scr[br, h] = m_new

    def finish(br, h):
        acc = acc_scr[br, h]
        return acc * (1.0 / pltpu.roll(acc, NSA_DH, axis=1))

    qbs = []
    for h in range(NSA_KVH):
        qs = []
        for g in range(NSA_G):
            c0 = (h * NSA_G + g) * NSA_DH
            qg = q_ref[0, :, c0:c0 + NSA_DH]
            qq = jnp.concatenate([qg, qg], axis=1)
            qs.append(jnp.where((lane_q // NSA_DH) == h, qq, 0.0))
        qbs.append(jnp.concatenate(qs, axis=0).astype(BF16))
        slope = slope_ref[h]
        feat_w = jnp.where(lane_m < ns, slope * rel_blk, jnp.where(lane_m == ns, slope, 0.0))
        qw_scr[h] = jnp.concatenate([qbs[h], feat_w.astype(BF16)], axis=1)

    lane_v2 = lax.broadcasted_iota(jnp.int32, (2 * tk, KV_LANES), 1)
    second = lax.broadcasted_iota(jnp.int32, (1, 2 * tk), 1) >= tk

    def tile2(br, ja, jb, k_ref, v_ref, q_scr, pen_b):
        ka = pl.multiple_of(ja * tk, tk)
        kb = pl.multiple_of(jb * tk, tk)
        kt = jnp.concatenate(
            [jnp.concatenate([k_ref[0, pl.ds(k0, tk), :].astype(BF16),
                              kfeat_ref[pl.ds(k0, tk), :]], axis=1) for k0 in (ka, kb)], axis=0)
        v_rows = jnp.concatenate([v_ref[0, pl.ds(ka, tk), :], v_ref[0, pl.ds(kb, tk), :]], axis=0)
        pen = jnp.where(second, pen_b, 0.0)
        for h in range(NSA_KVH):
            s = lax.dot_general(q_scr[h], kt, (((1,), (1,)), ((), ())),
                                preferred_element_type=F32) + pen
            vt = jnp.where((lane_v2 // NSA_DH) == h, v_rows, 1.0).astype(BF16)
            m_old = m_scr[br, h]
            m_new = jnp.maximum(m_old, jnp.max(s, axis=-1, keepdims=True))
            p = jnp.exp(s - jnp.concatenate([m_new] * (2 * tk // LANES), axis=1))
            acc_scr[br, h] = (jnp.exp(m_old - m_new) * acc_scr[br, h]
                              + jnp.dot(p.astype(BF16), vt, preferred_element_type=F32))
            m_scr[br, h] = m_new

    for d in range(nwin + 1):
        pen = None if d == 0 else jnp.where(i >= d, 0.0, NEG)
        mask = "causal" if d == 0 else ("far" if d == nwin else None)
        for h in range(NSA_KVH):
            tile(WIN, h, jnp.maximum(i - d, 0), kw_ref, vw_ref, qw_scr, mask, pen)

    lowest = None
    for h in range(NSA_KVH):
        qb = qbs[h]

        d_c = dc_ref[...] + t0.astype(F32)
        feat_c = jnp.where(lane_m == 0, slope_ref[h] * float(CMP_STRIDE), 0.0).astype(BF16)
        kc_aug = jnp.concatenate([kc_ref[0, 0].astype(BF16), cfeat_ref[...]], axis=1)
        s_c = lax.dot_general(jnp.concatenate([qb, feat_c], axis=1), kc_aug,
                              (((1,), (1,)), ((), ())), preferred_element_type=F32)
        s_c = jnp.where(d_c >= 0.0, s_c, NEG)
        mx = jnp.max(s_c, axis=-1, keepdims=True)
        p = jnp.where(s_c > 0.5 * NEG, jnp.exp(s_c - mx), 0.0)
        p_c = p * (1.0 / jnp.maximum(jnp.sum(p, axis=-1, keepdims=True), 1e-30))
        oc_scr[h] = _dot(p_c, vc_ref[0, 0])

        p_sum = p_c[0:tq]
        for g in range(1, NSA_G):
            p_sum = p_sum + p_c[g * tq:(g + 1) * tq]
        imp = _dot_exact_rhs(p_sum, mmap_ref[...])
        blk = lax.broadcasted_iota(jnp.int32, (tq, ns), 1)
        cur = (t0 + lax.broadcasted_iota(jnp.int32, (tq, ns), 0)) // SLC_BLOCK
        forced = (blk == 0) | (blk == cur) | (blk == cur - 1)
        score = jnp.where(blk <= cur, jnp.where(forced, SEL_FORCE, imp), -SEL_FORCE)
        score = jnp.concatenate([score, jnp.full((tq, LANES - ns), PAD_SCORE, F32)], axis=1)
        ns8 = -(-ns // 8) * 8
        sel_t = _select_top(score.T[0:ns8], ns, n_sel)
        blk_t = lax.broadcasted_iota(jnp.int32, (ns8, tq), 0).astype(F32)
        low = jnp.where(sel_t > 0.5, jnp.where(blk_t >= float(tk // SLC_BLOCK), blk_t, float(ns8)),
                        float(ns8))
        lowest = low if lowest is None else jnp.minimum(lowest, low)
        sel = jnp.concatenate([sel_t, jnp.zeros((LANES - ns8, tq), F32)], axis=0).T
        sel = jnp.concatenate([sel] * NSA_G, axis=0)

        feat_w = qw_scr[h, :, LANES:2 * LANES].astype(F32)
        feat_s = jnp.where(lane_m < ns, jnp.where(sel > 0.5, feat_w, -MASK_BIAS), feat_w)
        qs_scr[h] = jnp.concatenate([qb, feat_s.astype(BF16)], axis=1)

    for h in range(NSA_KVH):
        tile(SEL, h, i, ks_ref, vs_ref, qs_scr, "causal", None)

    lo = jnp.minimum((jnp.min(lowest) * (SLC_BLOCK / tk)).astype(jnp.int32), i)
    n_mid = i - lo
    n_tiles = n_mid + jnp.where(i >= 1, 1, 0)

    def sel_body(jj, carry):
        ia = 2 * jj
        ib = ia + 1
        ja = jnp.where(ia < n_mid, i - 1 - ia, 0)
        jb = jnp.where(ib < n_mid, i - 1 - ib, 0)
        pen_b = jnp.where(ib < n_tiles, 0.0, NEG)
        tile2(SEL, ja, jb, ks_ref, vs_ref, qs_scr, pen_b)
        return carry

    lax.fori_loop(0, (n_tiles + 1) // 2, sel_body, 0)
    o_s = [finish(SEL, h) for h in range(NSA_KVH)]
    o_w = [finish(WIN, h) for h in range(NSA_KVH)]

    gates = g_ref[0]
    out_cols = []
    for h in range(NSA_KVH):
        o_c = oc_scr[h]
        for g in range(NSA_G):
            col = GATE_COL0 + (h * NSA_G + g) * 3
            rs = slice(g * tq, (g + 1) * tq)
            hs = slice(h * NSA_DH, (h + 1) * NSA_DH)
            out_cols.append(gates[:, col:col + 1] * o_c[rs, hs]
                            + gates[:, col + 1:col + 2] * o_s[h][rs, hs]
                            + gates[:, col + 2:col + 3] * o_w[h][rs, hs])
    o_ref[0] = jnp.concatenate(out_cols, axis=1)


def _alibi_slopes():
    h = np.arange(1, NSA_HEADS + 1, dtype=np.float32)
    return np.exp2(-8.0 * h / NSA_HEADS).astype(np.float32).reshape(NSA_KVH, NSA_G)


def _nsa_prompt(qn, kcvc, rows, winr, gts):
    bsz, t, _ = qn.shape
    tq = min(t, 256)
    assert t % tq == 0 and WINDOW % tq == 0 and tq % SLC_BLOCK == 0 and tq % LANES == 0
    nch = kcvc.shape[2]
    ns = -(-t // SLC_BLOCK)
    assert ns < LANES
    n_sel = min(SLC_TOPN, ns)
    m_rows = NSA_G * tq
    slopes = _alibi_slopes()
    qi = np.tile(np.arange(tq, dtype=np.float32), NSA_G)
    slope_rows = np.repeat(slopes, tq, axis=1)
    slope_mat = np.ascontiguousarray(np.broadcast_to(slope_rows[:, :, None], (NSA_KVH, m_rows, LANES)))
    dq_mat = qi[:, None] - np.arange(tq, dtype=np.float32)[None, :]
    assert nch <= 2 * LANES
    cfeat = np.zeros((nch, LANES), np.float32)
    cfeat[:, 0] = np.arange(nch)
    cmp_end = np.arange(nch, dtype=np.float32) * CMP_STRIDE + (CMP_LEN - 1)
    dc_mat = qi[:, None] - cmp_end[None, :]
    mmap = jnp.asarray(_cmp_to_slc_map(nch, ns), BF16)
    key = np.arange(t)
    kfeat = np.zeros((t, LANES), np.float32)
    kfeat[key, key // SLC_BLOCK] = 1.0
    kfeat[:, ns] = key % SLC_BLOCK

    kv = lambda c: pl.BlockSpec((1, t, KV_LANES), lambda b, i: (b, 0, c))
    cst = lambda a: pl.BlockSpec(a.shape, lambda b, i: (0,) * a.ndim)
    consts = [jnp.asarray(slope_mat), jnp.asarray(dq_mat), jnp.asarray(cfeat, BF16),
              jnp.asarray(dc_mat), mmap, jnp.asarray(kfeat, BF16)]
    return pl.pallas_call(
        functools.partial(_nsa_prompt_body, tq=tq, ns=ns, n_sel=n_sel),
        grid=(bsz, t // tq),
        in_specs=[pl.BlockSpec((1, tq, NSA_Q), lambda b, i: (b, i, 0)),
                  pl.BlockSpec((1, 1, nch, KV_LANES), lambda b, i: (0, b, 0, 0)),
                  pl.BlockSpec((1, 1, nch, KV_LANES), lambda b, i: (1, b, 0, 0)),
                  kv(2), kv(3), kv(0), kv(1),
                  pl.BlockSpec((1, tq, LANES), lambda b, i: (b, i, 0))] + [cst(a) for a in consts],
        out_specs=pl.BlockSpec((1, tq, NSA_Q), lambda b, i: (b, i, 0)),
        out_shape=_sds((bsz, t, NSA_Q)),
        scratch_shapes=[pltpu.VMEM((NSA_KVH, m_rows, 2 * LANES), BF16),
                        pltpu.VMEM((NSA_KVH, m_rows, 2 * LANES), BF16),
                        pltpu.VMEM((2, NSA_KVH, m_rows, LANES), F32),
                        pltpu.VMEM((2, NSA_KVH, m_rows, KV_LANES), F32),
                        pltpu.VMEM((NSA_KVH, m_rows, KV_LANES), F32)],
        compiler_params=_params(("arbitrary", "arbitrary")),
        name="nsa_prompt",
    )(qn, kcvc, kcvc, rows, rows, winr, winr, gts, *consts)


def _nsa_sample_body(pt_ref, q_ref, kc_ref, vc_ref, rows_ref, winr_ref, swt_ref, g_ref,
                     slope_ref, pos_ref, cur_ref, dc_ref, gsum_ref, mmap_ref, expand_ref,
                     kfeat_ref, cache_ref, o_ref, wout_ref, kvbuf, sem, *, past, t, npages, ns, n_sel):
    b = pl.program_id(0)
    nb = pl.num_programs(0)
    page = past // npages
    m_rows = NSA_HEADS * t
    half = m_rows // NSA_KVH
    pw = swt_ref.shape[-1]

    def page_copy(bb, slot, p):
        return pltpu.make_async_copy(cache_ref.at[pt_ref[bb, p], pl.ds(2, 2)],
                                     kvbuf.at[slot, :, :, pl.ds(p * page, page)], sem.at[slot])

    @pl.when(b == 0)
    def _():
        for p in range(npages):
            page_copy(0, 0, p).start()

    @pl.when(b + 1 < nb)
    def _():
        for p in range(npages):
            page_copy(b + 1, (b + 1) % 2, p).start()

    pieces = []
    for hh in range(NSA_HEADS):
        qg = q_ref[0, :, hh * NSA_DH:(hh + 1) * NSA_DH]
        z = jnp.zeros_like(qg)
        pieces.append(jnp.concatenate([qg, z] if hh < NSA_G else [z, qg], axis=1))
    qbd = jnp.concatenate(pieces, axis=0).astype(BF16)
    slope = slope_ref[...]
    pos = pos_ref[...]

    def widen(x, n):
        return jnp.concatenate([x] * (n // LANES), axis=1)

    def softmax(parts):
        mx = None
        for s in parts:
            r = jnp.max(s, axis=-1, keepdims=True)
            mx = r if mx is None else jnp.maximum(mx, r)
        ps = [jnp.where(s > 0.5 * NEG, jnp.exp(s - mx), 0.0) for s in parts]
        tot = None
        for p in ps:
            r = jnp.sum(p, axis=-1, keepdims=True)
            tot = r if tot is None else tot + r
        inv = 1.0 / jnp.maximum(tot, 1e-30)
        return [p * inv for p in ps]

    d_c = dc_ref[...]
    nch = d_c.shape[1]
    s_c = _dot_nt(qbd, kc_ref[0, 0])
    (p_c,) = softmax([jnp.where(d_c >= 0.0, s_c - widen(slope, nch) * d_c, NEG)])
    o_c = _dot(p_c, vc_ref[0, 0])

    imp = _dot_exact_lhs(gsum_ref[...], _dot_exact_rhs(p_c, mmap_ref[...]))
    ns8 = imp.shape[1]
    imp = jnp.concatenate([imp, jnp.zeros((m_rows, 2 * LANES - ns8), F32)], axis=1)
    imp = jnp.concatenate([imp, jnp.zeros((LANES - m_rows, 2 * LANES), F32)], axis=0)
    imp_t = imp.T[0:ns8, 0:m_rows]
    blk = lax.broadcasted_iota(jnp.int32, (ns8, m_rows), 0)
    cur = cur_ref[...]
    forced = (blk == 0) | (blk == cur) | (blk == cur - 1)
    score = jnp.where(blk <= cur, jnp.where(forced, SEL_FORCE, imp_t), -SEL_FORCE)
    score = jnp.where(blk < ns, score, PAD_SCORE)
    sel_t = _select_top(score, ns, n_sel)
    sel_t = jnp.concatenate([sel_t, jnp.zeros((ns8, LANES - m_rows), F32)], axis=1)
    sel_t = jnp.concatenate([sel_t, jnp.zeros((2 * LANES - ns8, LANES), F32)], axis=0)
    sel = sel_t.T[0:m_rows, 0:ns8]

    new_ks = rows_ref[0, :, 2 * KV_LANES:3 * KV_LANES]
    new_vs = rows_ref[0, :, 3 * KV_LANES:4 * KV_LANES]
    new_kw = winr_ref[0, :, 0:KV_LANES]
    new_vw = winr_ref[0, :, KV_LANES:2 * KV_LANES]
    dist_n = pos[:, 0:t] - (past + lax.broadcasted_iota(jnp.int32, (m_rows, t), 1)).astype(F32)
    slope_n = slope[:, 0:t]

    kwt = swt_ref[0, 0]
    vwt = swt_ref[0, 1]
    dist_w = widen(pos, pw) - (lax.broadcasted_iota(jnp.int32, (m_rows, pw), 1)
                               + (past - pw)).astype(F32)
    s_w = _dot(qbd, kwt)
    s_w = jnp.where(dist_w >= 0.0,
                    jnp.where(dist_w <= float(WINDOW), s_w - widen(slope, pw) * dist_w, NEG), NEG)
    s_wn = _dot_nt(qbd, new_kw)
    s_wn = jnp.where(dist_n >= 0.0,
                     jnp.where(dist_n <= float(WINDOW), s_wn - slope_n * dist_n, NEG), NEG)
    p_w, p_wn = softmax([s_w, s_wn])
    o_w = _dot_nt(p_w, vwt) + _dot(p_wn, new_vw)
    w_rows = jnp.concatenate([jnp.concatenate([kwt.T, vwt.T], axis=1),
                              winr_ref[0]], axis=0)
    wout_ref[0] = w_rows[pw + t - wout_ref.shape[1]:, :]

    slot = b % 2
    for p in range(npages):
        page_copy(b, slot, p).wait()
    nfull = past // SLC_BLOCK
    lane_f = lax.broadcasted_iota(jnp.int32, (m_rows, 16), 1)
    slope_f = slope[:, 0:16]
    q_feat = jnp.where(lane_f == 0, slope_f * float(SLC_BLOCK),
                       jnp.where(lane_f == 1, slope_f, 0.0)).astype(BF16)
    sel_bias = jnp.where(sel > 0.5, 0.0, -MASK_BIAS).astype(BF16)
    s_s = (_dot(qbd, kvbuf[slot, 0])
           + jnp.dot(sel_bias, expand_ref[...], preferred_element_type=F32)
           + jnp.dot(q_feat, kfeat_ref[...], preferred_element_type=F32))
    row_n = lax.broadcasted_iota(jnp.int32, (m_rows, t), 1).astype(F32)
    s_sn = _dot_nt(qbd, new_ks)
    s_sn = jnp.where(sel[:, nfull:nfull + 1] > 0.5,
                     jnp.where(dist_n >= 0.0, s_sn + slope_n * row_n, NEG), NEG)
    p_s, p_sn = softmax([s_s, s_sn])
    o_s = _dot_nt(p_s, kvbuf[slot, 1]) + _dot(p_sn, new_vs)

    def own(x):
        return jnp.concatenate([x[0:half, 0:NSA_DH], x[half:, NSA_DH:2 * NSA_DH]], axis=0)

    gates = g_ref[0]

    def gate_col(br):
        return jnp.concatenate(
            [gates[:, GATE_COL0 + hh * 3 + br:GATE_COL0 + hh * 3 + br + 1] for hh in range(NSA_HEADS)],
            axis=0)

    mix = gate_col(0) * own(o_c) + gate_col(1) * own(o_s) + gate_col(2) * own(o_w)
    o_ref[0] = jnp.concatenate([mix[hh * t:(hh + 1) * t, :] for hh in range(NSA_HEADS)], axis=1)


def _nsa_sample(qn, kcvc, rows, winr, state_win_t, gts, cache_t, page_table):
    bsz, t, _ = qn.shape
    npages = page_table.shape[1]
    page = cache_t.shape[-1]
    past = npages * page
    pw = state_win_t.shape[-1]
    nch = kcvc.shape[2]
    ns = -(-(past + t) // SLC_BLOCK)
    ns8 = -(-ns // 8) * 8
    m_rows = NSA_HEADS * t
    assert past % SLC_BLOCK == 0 and t <= SLC_BLOCK and t % 8 == 0 and pw + t >= WINDOW
    assert m_rows <= LANES and ns8 <= 2 * LANES and nch % LANES == 0 and pw % LANES == 0
    n_sel = min(SLC_TOPN, ns)
    slopes = _alibi_slopes().reshape(-1)
    lanes1 = np.ones((1, LANES), np.float32)
    slope_b = np.repeat(slopes, t)[:, None].astype(np.float32) * lanes1
    pos_i = np.tile(past + np.arange(t), NSA_HEADS)
    cmp_end = np.arange(nch, dtype=np.float32) * CMP_STRIDE + (CMP_LEN - 1)
    dc_mat = pos_i[:, None].astype(np.float32) - cmp_end[None, :]
    row_kvh = np.arange(m_rows) // (NSA_G * t)
    row_tok = np.arange(m_rows) % t
    gsum = ((row_kvh[:, None] == row_kvh[None, :]) & (row_tok[:, None] == row_tok[None, :]))
    mmap = np.zeros((nch, ns8), np.float32)
    mmap[:, :ns] = _cmp_to_slc_map(nch, ns)
    expand = (np.arange(ns8)[:, None] == (np.arange(past) // SLC_BLOCK)[None, :])
    assert past // SLC_BLOCK <= 2 * LANES
    kfeat = np.zeros((16, past), np.float32)
    kfeat[0] = np.arange(past) // SLC_BLOCK - past // SLC_BLOCK
    kfeat[1] = np.arange(past) % SLC_BLOCK
    consts = [jnp.asarray(slope_b), jnp.asarray(pos_i[:, None].astype(np.float32) * lanes1),
              jnp.asarray((pos_i // SLC_BLOCK)[None, :].astype(np.int32)), jnp.asarray(dc_mat),
              jnp.asarray(gsum.astype(np.float32), BF16), jnp.asarray(mmap, BF16),
              jnp.asarray(expand.astype(np.float32), BF16), jnp.asarray(kfeat, BF16)]

    cst = lambda a: pl.BlockSpec(a.shape, lambda b, pt: (0,) * a.ndim)
    grid_spec = pltpu.PrefetchScalarGridSpec(
        num_scalar_prefetch=1,
        grid=(bsz,),
        in_specs=[pl.BlockSpec((1, t, NSA_Q), lambda b, pt: (b, 0, 0)),
                  pl.BlockSpec((1, 1, nch, KV_LANES), lambda b, pt: (0, b, 0, 0)),
                  pl.BlockSpec((1, 1, nch, KV_LANES), lambda b, pt: (1, b, 0, 0)),
                  pl.BlockSpec((1, t, ROWS_LANES), lambda b, pt: (b, 0, 0)),
                  pl.BlockSpec((1, t, 2 * KV_LANES), lambda b, pt: (b, 0, 0)),
                  pl.BlockSpec((1, 2, KV_LANES, pw), lambda b, pt: (b, 0, 0, 0)),
                  pl.BlockSpec((1, t, LANES), lambda b, pt: (b, 0, 0))]
        + [cst(a) for a in consts] + [pl.BlockSpec(memory_space=pl.ANY)],
        out_specs=[pl.BlockSpec((1, t, NSA_Q), lambda b, pt: (b, 0, 0)),
                   pl.BlockSpec((1, WINDOW, 2 * KV_LANES), lambda b, pt: (b, 0, 0))],
        scratch_shapes=[pltpu.VMEM((2, 2, KV_LANES, past), F32), pltpu.SemaphoreType.DMA((2,))],
    )
    return pl.pallas_call(
        functools.partial(_nsa_sample_body, past=past, t=t, npages=npages, ns=ns, n_sel=n_sel),
        grid_spec=grid_spec,
        out_shape=[_sds((bsz, t, NSA_Q)), _sds((bsz, WINDOW, 2 * KV_LANES))],
        compiler_params=_params(("arbitrary",)),
        name="nsa_sample",
    )(page_table, qn, kcvc, kcvc, rows, winr, state_win_t, gts, *consts, cache_t)


def _ffn_body(x_ref, og_ref, on_ref, gtm_ref, scf_ref, shf_ref, gtf_ref,
              nmp_ref, nfp_ref, nfo_ref, wo_ref, wu_ref, wd_ref, y_ref, *, ff_chunk):
    mix = (jnp.dot(og_ref[0].astype(BF16), wo_ref[0:GLA_V, :], preferred_element_type=F32)
           + jnp.dot(on_ref[0].astype(BF16), wo_ref[GLA_V:GLA_V + NSA_Q, :],
                     preferred_element_type=F32))
    x1 = x_ref[0] + gtm_ref[0] * _rms(mix, nmp_ref[...])
    hb = (_rms(x1, nfp_ref[...]) * (1.0 + scf_ref[0]) + shf_ref[0]).astype(BF16)
    f = None
    for c in range(wu_ref.shape[1] // ff_chunk):
        cs = slice(c * ff_chunk, (c + 1) * ff_chunk)
        u = jnp.maximum(jnp.dot(hb, wu_ref[:, cs], preferred_element_type=F32), 0.0)
        term = jnp.dot((u * u).astype(BF16), wd_ref[cs, :], preferred_element_type=F32)
        f = term if f is None else f + term
    y_ref[0] = x1 + gtf_ref[0] * _rms(f, nfo_ref[...])


def _ffn(x, og, on, gtm, scf, shf, gtf, nmp, nfp, nfo, wo, wu, wd):
    bsz, t, d = x.shape
    tq = min(t, 512)
    assert t % tq == 0
    tok = lambda n: pl.BlockSpec((1, tq, n), lambda b, i: (b, i, 0))
    full = lambda a: pl.BlockSpec(a.shape, lambda b, i: (0,) * a.ndim,
                                  pipeline_mode=pl.Buffered(1))
    vec = lambda a: pl.BlockSpec(a.shape, lambda b, i: (0,) * a.ndim)
    return pl.pallas_call(
        functools.partial(_ffn_body, ff_chunk=min(wu.shape[1], 1024)),
        grid=(bsz, t // tq),
        in_specs=[tok(d), tok(GLA_V), tok(NSA_Q), _mod_spec(gtm, tq), _mod_spec(scf, tq),
                  _mod_spec(shf, tq), _mod_spec(gtf, tq), vec(nmp), vec(nfp), vec(nfo),
                  full(wo), full(wu), full(wd)],
        out_specs=tok(d),
        out_shape=_sds((bsz, t, d)),
        compiler_params=_params(("arbitrary", "arbitrary")),
        name="ffn",
    )(x, og, on, gtm, scf, shf, gtf, nmp, nfp, nfo, wo, wu, wd)


def _permute_w_in(w_in):
    o_zq = 0
    o_zk = o_zq + GLA_QK
    o_zv = o_zk + GLA_QK
    o_za = o_zv + GLA_V
    o_zr = o_za + GLA_RANK
    o_zqn = o_zr + GLA_V
    o_zkv = o_zqn + NSA_Q
    o_zg = o_zkv + NSA_KV
    o_end = o_zg + NSA_GATE
    d = w_in.shape[0]
    pad = jnp.zeros((d, LANES - GLA_RANK - NSA_GATE), w_in.dtype)
    cols = [w_in[:, o_zq:o_za], w_in[:, o_zr:o_zg], w_in[:, o_za:o_zr], w_in[:, o_zg:o_end], pad]
    return jnp.concatenate(cols, axis=1).astype(BF16)


def _layer_weights(l, norm_mix_pre, norm_mix_post, norm_ffn_pre, norm_ffn_post, w_in,
                   gla_w_gate, gla_b_gate, gla_norm, cmp_pos, cmp_w1, cmp_b1, cmp_w2, cmp_b2,
                   w_out, w_up, w_down):
    wg_pad = jnp.zeros((LANES, GLA_QK), F32).at[:GLA_RANK].set(gla_w_gate[l]).astype(BF16)
    w1, w2 = _compress_weights(cmp_w1[l], cmp_w2[l])
    return dict(
        nmpre=norm_mix_pre[l][None, :], nmpost=norm_mix_post[l][None, :],
        nfpre=norm_ffn_pre[l][None, :], nfpost=norm_ffn_post[l][None, :],
        w_perm=_permute_w_in(w_in[l]), wg_pad=wg_pad, bg=gla_b_gate[l][None, :],
        gnorm=gla_norm[l][None, :], w1=w1, w2=w2, posb=_posbias(cmp_pos[l], cmp_w1[l]),
        b1=cmp_b1[l][:, None, :], b2=cmp_b2[l][:, None, :],
        wo=w_out[l].astype(BF16), wu=w_up[l].astype(BF16), wd=w_down[l].astype(BF16))


def _split_ada(ada, rows_per_batch):
    parts = jnp.split(ada, 6, axis=-1)
    if rows_per_batch is None:
        return [p[:, None, :] for p in parts]
    return [jnp.repeat(p, rows_per_batch, axis=0)[None] for p in parts]


def _prompt_layer(x, ada, w, wbuf):
    bsz, t, d = x.shape
    sh_m, sc_m, gt_m, sh_f, sc_f, gt_f = _split_ada(ada, None)
    qg, kg, vg, rg, qn, rows, winr, logf, gts = _inproj(x, sc_m, sh_m, w["nmpre"], w["w_perm"],
                                                        w["wg_pad"], w["bg"])
    s0 = jnp.zeros((bsz, GLA_HEADS, GLA_DK, GLA_DV), F32)
    og, s_fin = _gla(qg, kg, vg, logf, rg, s0, w["gnorm"])
    kcvc = _compress_prompt(rows, w["w1"], w["posb"], w["b1"], w["w2"], w["b2"])
    on = _nsa_prompt(qn, kcvc, rows, winr, gts)
    y = _ffn(x, og, on, gt_m, sc_f, sh_f, gt_f, w["nmpost"], w["nfpre"], w["nfpost"],
             w["wo"], w["wu"], w["wd"])
    win = winr.reshape(bsz, t, 2, NSA_KVH, NSA_DH)
    if t < wbuf:
        win = jnp.pad(win, ((0, 0), (wbuf - t, 0), (0, 0), (0, 0), (0, 0)))
    return y, rows.reshape(bsz, t, 4, NSA_KVH, NSA_DH), win[:, -wbuf:], s_fin


def _sample_layer(x, ada, cache, page_table, state_win, state_gla, w):
    bsz, t, d = x.shape
    n = bsz * t
    sh_m, sc_m, gt_m, sh_f, sc_f, gt_f = _split_ada(ada, t)
    outs = _inproj(x.reshape(1, n, d), sc_m, sh_m, w["nmpre"], w["w_perm"], w["wg_pad"], w["bg"])
    qg, kg, vg, rg, qn, rows, winr, logf, gts = [o.reshape(bsz, t, o.shape[-1]) for o in outs]
    og, s_fin = _gla(qg, kg, vg, logf, rg, state_gla, w["gnorm"])
    n_pool, page = cache.shape[0], cache.shape[1]
    cache_t = jnp.transpose(cache, (0, 2, 3, 4, 1)).reshape(n_pool, 4, KV_LANES, page)
    pw = state_win.shape[1]
    state_win_t = jnp.transpose(state_win, (0, 2, 3, 4, 1)).reshape(bsz, 2, KV_LANES, pw)
    kcvc = _compress_paged(cache_t, page_table, w["w1"], w["posb"], w["b1"], w["w2"], w["b2"])
    on, win = _nsa_sample(qn, kcvc, rows, winr, state_win_t, gts, cache_t, page_table)
    y = _ffn(x.reshape(1, n, d), og.reshape(1, n, GLA_V), on.reshape(1, n, NSA_Q),
             gt_m, sc_f, sh_f, gt_f, w["nmpost"], w["nfpre"], w["nfpost"],
             w["wo"], w["wu"], w["wd"])
    return (y.reshape(bsz, t, d), rows.reshape(bsz, t, 4, NSA_KVH, NSA_DH),
            win.reshape(bsz, WINDOW, 2, NSA_KVH, NSA_DH), s_fin)


def kernel(x_prompt, x_sample, cache_kv, state_win, state_gla, page_table, c_prompt, c_sample,
           norm_mix_pre, norm_mix_post, norm_ffn_pre, norm_ffn_post, w_ada, b_ada, w_in,
           gla_w_gate, gla_b_gate, gla_norm, cmp_pos, cmp_w1, cmp_b1, cmp_w2, cmp_b2,
           w_out, w_up, w_down):
    depth = w_in.shape[0]
    bsz = x_prompt.shape[0]
    wbuf = state_win.shape[2]
    assert wbuf == WINDOW
    c_all = jnp.concatenate([c_prompt, c_sample], axis=0)
    y_p, y_s = x_prompt, x_sample
    outs = [[] for _ in range(6)]
    for l in range(depth):
        w = _layer_weights(l, norm_mix_pre, norm_mix_post, norm_ffn_pre, norm_ffn_post, w_in,
                           gla_w_gate, gla_b_gate, gla_norm, cmp_pos, cmp_w1, cmp_b1, cmp_w2,
                           cmp_b2, w_out, w_up, w_down)
        ada = _ada(c_all, w_ada[l], b_ada[l])
        y_p, r_p, w_p, s_p = _prompt_layer(y_p, ada[:bsz], w, wbuf)
        y_s, r_s, w_s, s_s = _sample_layer(y_s, ada[bsz:], cache_kv[l], page_table,
                                           state_win[l], state_gla[l], w)
        for lst, val in zip(outs, (r_p, r_s, w_p, w_s, s_p, s_s)):
            lst.append(val)
    return (y_p, y_s) + tuple(jnp.stack(o) for o in outs)
```
